```python
import math
import jax, jax.numpy as jnp
from jax import lax
import numpy as np

D_MODEL = 1024
BATCH = 8
SEQ = 2048
DEPTH = 4
DEC_BATCH = 128
DEC_SEQ = 4
PAST_LEN = 2048
PAGE_SIZE = 128

SSM_EXPAND = 2
D_INNER = SSM_EXPAND * D_MODEL
SSM_HEAD_DIM = 64
SSM_HEADS = D_INNER // SSM_HEAD_DIM
SSM_GROUPS = 8
SSM_STATE = 128
CONV_WIDTH = 4
CONV_DIM = D_INNER + 2 * SSM_GROUPS * SSM_STATE
SSD_CHUNK = 128
ATTN_HEAD_DIM = 64
DIL_PATTERNS = ((128, 1), (512, 4), (2048, 16))
HEADS_PER_GROUP = 4
ATTN_HEADS = HEADS_PER_GROUP * len(DIL_PATTERNS)
ATTN_WIDTH = ATTN_HEADS * ATTN_HEAD_DIM
ATTN_OUT_WIDTH = HEADS_PER_GROUP * ATTN_HEAD_DIM
ROPE_DIM = ATTN_HEAD_DIM // 4
ROPE_THETA = 500000.0
Q_BLOCK = 128
D_FF = 4 * D_MODEL
DEEPNORM_ALPHA = (2 * DEPTH) ** 0.25
DEEPNORM_BETA = (8 * DEPTH) ** -0.25
LN_EPS = 1e-5
RMS_EPS = 1e-5
OFF_XBC = D_INNER
OFF_DT = OFF_XBC + CONV_DIM
OFF_Q = OFF_DT + SSM_HEADS
OFF_K = OFF_Q + ATTN_WIDTH
OFF_V = OFF_K + ATTN_WIDTH
OFF_G = OFF_V + ATTN_WIDTH
D_IN_PROJ = OFF_G + 2 * D_MODEL

kernel_name = 'hybrid_ssd_dilated_attn_deepnorm_step'


def layer_norm(x, g, b):
    xf = x.astype(jnp.float32)
    mu = jnp.mean(xf, -1, keepdims=True)
    var = jnp.mean(jnp.square(xf - mu), -1, keepdims=True)
    return ((xf - mu) * lax.rsqrt(var + LN_EPS) * g.astype(jnp.float32) + b.astype(jnp.float32)).astype(x.dtype)


def apply_partial_rotary(x, pos):
    half = ROPE_DIM // 2
    inv_freq = jnp.power(ROPE_THETA, -jnp.arange(half, dtype=jnp.float32) / half)
    ang = pos.astype(jnp.float32)[:, None] * inv_freq[None, :]
    cos = jnp.cos(ang)[None, :, None, :]
    sin = jnp.sin(ang)[None, :, None, :]
    xr = x[..., :ROPE_DIM].astype(jnp.float32)
    x1, x2 = xr[..., :half], xr[..., half:]
    rot = jnp.concatenate([x1 * cos - x2 * sin, x2 * cos + x1 * sin], -1).astype(x.dtype)
    return jnp.concatenate([rot, x[..., ROPE_DIM:]], -1)


def dilated_group_attention(q, kv, q_offset, dilation, n_keys):
    bsz, lq, nh, hd = q.shape
    qb = min(Q_BLOCK, lq)
    nb = lq // qb
    scale = 1.0 / math.sqrt(hd)
    q_blocks = q.reshape(bsz, nb, qb, nh, hd).swapaxes(0, 1)
    steps = dilation * jnp.arange(n_keys)

    def one_block(args):
        qblk, blk = args
        qi = q_offset + blk * qb + jnp.arange(qb)
        idx = qi[:, None] - steps[None, :]
        valid = idx >= 0
        kvg = jnp.take(kv, jnp.maximum(idx, 0), axis=1).astype(jnp.float32)
        s = jnp.einsum('bqhd,bqkhd->bqhk', qblk.astype(jnp.float32), kvg[:, :, :, 0]) * scale
        s = jnp.where(valid[None, :, None, :], s, -jnp.inf)
        lse = jax.nn.logsumexp(s, axis=-1)
        p = jnp.exp(s - lse[..., None])
        o = jnp.einsum('bqhk,bqkhd->bqhd', p, kvg[:, :, :, 1])
        return o, lse

    o, lse = lax.map(one_block, (q_blocks, jnp.arange(nb)))
    o = o.swapaxes(0, 1).reshape(bsz, lq, nh, hd)
    lse = lse.swapaxes(0, 1).reshape(bsz, lq, nh)
    return o, lse


def dilated_mixture_attention(q, k, v, pos, kv_bufs):
    bsz, seqlen = q.shape[:2]
    q = apply_partial_rotary(q, pos)
    k = apply_partial_rotary(k, pos)
    outs, lses, kv_rows = [], [], []
    for j, (window, dilation) in enumerate(DIL_PATTERNS):
        hs = slice(j * HEADS_PER_GROUP, (j + 1) * HEADS_PER_GROUP)
        kv_g = jnp.stack([k[:, :, hs], v[:, :, hs]], axis=2)
        if kv_bufs is None:
            src, q_off = kv_g, 0
            kv_rows.append(kv_g[:, seqlen - min(window, seqlen):])
        else:
            buf = kv_bufs[j].astype(kv_g.dtype)
            src = jnp.concatenate([buf, kv_g], axis=1)
            q_off = buf.shape[1]
            kv_rows.append(kv_g)
        o, lse = dilated_group_attention(q[:, :, hs], src, q_off, dilation, window // dilation + 1)
        outs.append(o)
        lses.append(lse)
    wts = jax.nn.softmax(jnp.stack(lses), axis=0)
    o = jnp.einsum('gblh,gblhd->blhd', wts, jnp.stack(outs))
    return o.reshape(bsz, seqlen, ATTN_OUT_WIDTH).astype(q.dtype), kv_rows


def causal_depthwise_conv(xbc, buf, w, b):
    seqlen = xbc.shape[1]
    xp = jnp.concatenate([buf.astype(xbc.dtype), xbc], axis=1)
    out = b
    for tap in range(CONV_WIDTH):
        out = out + xp[:, tap:tap + seqlen] * w[tap]
    return jax.nn.silu(out), xp[:, seqlen:]


def ssd_chunked(x, dt, a, bm, cm, h0):
    bsz, seqlen, nh, hp = x.shape
    ng, ns = bm.shape[2], bm.shape[3]
    nr = nh // ng
    qc = min(SSD_CHUNK, seqlen)
    nc = seqlen // qc
    f32 = jnp.float32
    xc = x.reshape(bsz, nc, qc, ng, nr, hp).astype(f32)
    dtc = dt.reshape(bsz, nc, qc, ng, nr)
    bc = bm.reshape(bsz, nc, qc, ng, ns).astype(f32)
    cc = cm.reshape(bsz, nc, qc, ng, ns).astype(f32)
    acum = jnp.cumsum(dtc * a.reshape(ng, nr), axis=2)
    xdt = xc * dtc[..., None]
    causal = jnp.tril(jnp.ones((qc, qc), dtype=bool))[:, :, None, None]
    seg = acum[:, :, :, None] - acum[:, :, None, :]
    decay = jnp.exp(jnp.where(causal, seg, -jnp.inf))
    cb = jnp.einsum('bcign,bcjgn->bcijg', cc, bc)
    y_diag = jnp.einsum('bcijgr,bcjgrp->bcigrp', cb[..., None] * decay, xdt)
    decay_end = jnp.exp(acum[:, :, -1:] - acum)
    states = jnp.einsum('bcjgn,bcjgrp->bcgrpn', bc, xdt * decay_end[..., None])
    chunk_decay = jnp.exp(acum[:, :, -1])

    def step(h, inp):
        s, dec = inp
        return h * dec[..., None, None] + s, h

    h_init = h0.astype(f32).reshape(bsz, ng, nr, hp, ns)
    h_final, h_prev = lax.scan(step, h_init, (states.swapaxes(0, 1), chunk_decay.swapaxes(0, 1)))
    h_prev = h_prev.swapaxes(0, 1)
    y_off = jnp.einsum('bcign,bcgrpn->bcigrp', cc, h_prev) * jnp.exp(acum)[..., None]
    y = (y_diag + y_off).reshape(bsz, seqlen, nh, hp)
    return y, h_final.reshape(bsz, nh, hp, ns)


def ssd_mixer(z, xbc, dt_raw, conv_buf, h0, conv_w, conv_b, dt_bias, a_log, d_skip, ssm_norm_g):
    bsz, seqlen = z.shape[:2]
    xbc, conv_new = causal_depthwise_conv(xbc, conv_buf, conv_w, conv_b)
    gn = SSM_GROUPS * SSM_STATE
    xs = xbc[..., :D_INNER].reshape(bsz, seqlen, SSM_HEADS, SSM_HEAD_DIM)
    bm = xbc[..., D_INNER:D_INNER + gn].reshape(bsz, seqlen, SSM_GROUPS, SSM_STATE)
    cm = xbc[..., D_INNER + gn:].reshape(bsz, seqlen, SSM_GROUPS, SSM_STATE)
    dt = jax.nn.softplus(dt_raw.astype(jnp.float32) + dt_bias.astype(jnp.float32))
    a = -jnp.exp(a_log.astype(jnp.float32))
    y, h_new = ssd_chunked(xs, dt, a, bm, cm, h0)
    y = y + d_skip.astype(jnp.float32)[:, None] * xs.astype(jnp.float32)
    hgt = y.reshape(bsz, seqlen, D_INNER) * jax.nn.silu(z.astype(jnp.float32))
    hg = hgt.reshape(bsz, seqlen, SSM_GROUPS, D_INNER // SSM_GROUPS)
    hg = hg * lax.rsqrt(jnp.mean(hg * hg, -1, keepdims=True) + RMS_EPS)
    out = hg.reshape(bsz, seqlen, D_INNER) * ssm_norm_g.astype(jnp.float32)
    return out.astype(z.dtype), conv_new, h_new


def decoder_layer(x, pos0, kv_bufs, conv_buf, h0, w_in, conv_w, conv_b, dt_bias, a_log, d_skip,
                  ssm_norm_g, w_ssm_br, w_attn_br, w_o, ln1_g, ln1_b, w_up, w_down, ln2_g, ln2_b):
    bsz, seqlen, _ = x.shape
    proj = jnp.einsum('bld,de->ble', x, w_in)
    z = proj[..., :OFF_XBC]
    xbc = proj[..., OFF_XBC:OFF_DT]
    dt_raw = proj[..., OFF_DT:OFF_Q]
    q = proj[..., OFF_Q:OFF_K].reshape(bsz, seqlen, ATTN_HEADS, ATTN_HEAD_DIM)
    k = proj[..., OFF_K:OFF_V].reshape(bsz, seqlen, ATTN_HEADS, ATTN_HEAD_DIM)
    v = proj[..., OFF_V:OFF_G].reshape(bsz, seqlen, ATTN_HEADS, ATTN_HEAD_DIM)
    gate_ssm = jax.nn.sigmoid(proj[..., OFF_G:OFF_G + D_MODEL])
    gate_attn = jax.nn.sigmoid(proj[..., OFF_G + D_MODEL:])
    if conv_buf is None:
        conv_buf = jnp.zeros((bsz, CONV_WIDTH - 1, CONV_DIM), x.dtype)
        h0 = jnp.zeros((bsz, SSM_HEADS, SSM_HEAD_DIM, SSM_STATE), jnp.float32)
    y_ssm, conv_new, h_new = ssd_mixer(z, xbc, dt_raw, conv_buf, h0, conv_w, conv_b, dt_bias,
                                       a_log, d_skip, ssm_norm_g)
    pos = pos0 + jnp.arange(seqlen)
    o_attn, kv_rows = dilated_mixture_attention(q, k, v, pos, kv_bufs)
    merged = gate_ssm * jnp.einsum('ble,ed->bld', y_ssm, w_ssm_br) \
        + gate_attn * jnp.einsum('ble,ed->bld', o_attn, w_attn_br)
    mix = jnp.einsum('bld,de->ble', merged, w_o)
    x = layer_norm(DEEPNORM_ALPHA * x + mix, ln1_g, ln1_b)
    hid = jnp.square(jax.nn.relu(jnp.einsum('bld,df->blf', x, w_up)))
    x = layer_norm(DEEPNORM_ALPHA * x + jnp.einsum('blf,fd->bld', hid, w_down), ln2_g, ln2_b)
    return x, kv_rows, conv_new, h_new


def setup_inputs(seed: int = 0) -> dict:
    key = jax.random.key(seed)
    ks = jax.random.split(key, 24)
    f32 = jnp.float32
    nrm = lambda k, shape, s: jax.random.normal(k, shape, f32) * s
    win_lens = [min(w, PAST_LEN) for (w, _) in DIL_PATTERNS]
    col_scale = jnp.concatenate([jnp.ones((OFF_V,), f32),
                                 jnp.full((ATTN_WIDTH,), DEEPNORM_BETA, f32),
                                 jnp.ones((D_IN_PROJ - OFF_G,), f32)])
    w_in = nrm(ks[0], (DEPTH, D_MODEL, D_IN_PROJ), D_MODEL ** -0.5) * col_scale
    u = jax.random.uniform(ks[1], (DEPTH, SSM_HEADS), f32)
    dt0 = jnp.exp(u * (math.log(0.1) - math.log(0.001)) + math.log(0.001))
    dt_bias = dt0 + jnp.log(-jnp.expm1(-dt0))
    a_log = jnp.log(jax.random.uniform(ks[2], (DEPTH, SSM_HEADS), f32, 1.0, 16.0))
    return {
        'x_prompt': nrm(ks[3], (BATCH, SEQ, D_MODEL), 1.0),
        'x_sample': nrm(ks[4], (DEC_BATCH, DEC_SEQ, D_MODEL), 1.0),
        'cache_kv_w128': nrm(ks[5], (DEPTH, DEC_BATCH, win_lens[0], 2, HEADS_PER_GROUP, ATTN_HEAD_DIM), 1.0),
        'cache_kv_w512': nrm(ks[6], (DEPTH, DEC_BATCH, win_lens[1], 2, HEADS_PER_GROUP, ATTN_HEAD_DIM), 1.0),
        'cache_kv_w2048': nrm(ks[7], (DEPTH, DEC_BATCH, win_lens[2], 2, HEADS_PER_GROUP, ATTN_HEAD_DIM), 1.0),
        'state_conv': nrm(ks[8], (DEPTH, DEC_BATCH, CONV_WIDTH - 1, CONV_DIM), 1.0),
        'state_ssm': nrm(ks[9], (DEPTH, DEC_BATCH, SSM_HEADS, SSM_HEAD_DIM, SSM_STATE), 0.1),
        'w_in': w_in,
        'conv_w': nrm(ks[10], (DEPTH, CONV_WIDTH, CONV_DIM), CONV_WIDTH ** -0.5),
        'conv_b': nrm(ks[11], (DEPTH, CONV_DIM), 0.01),
        'dt_bias': dt_bias,
        'a_log': a_log,
        'd_skip': 1.0 + nrm(ks[12], (DEPTH, SSM_HEADS), 0.01),
        'ssm_norm_g': 1.0 + nrm(ks[13], (DEPTH, D_INNER), 0.01),
        'w_ssm_br': nrm(ks[14], (DEPTH, D_INNER, D_MODEL), D_INNER ** -0.5),
        'w_attn_br': nrm(ks[15], (DEPTH, ATTN_OUT_WIDTH, D_MODEL), ATTN_OUT_WIDTH ** -0.5),
        'w_o': nrm(ks[16], (DEPTH, D_MODEL, D_MODEL), D_MODEL ** -0.5 * DEEPNORM_BETA),
        'ln1_g': 1.0 + nrm(ks[17], (DEPTH, D_MODEL), 0.01),
        'ln1_b': nrm(ks[18], (DEPTH, D_MODEL), 0.01),
        'w_up': nrm(ks[19], (DEPTH, D_MODEL, D_FF), D_MODEL ** -0.5),
        'w_down': nrm(ks[20], (DEPTH, D_FF, D_MODEL), D_FF ** -0.5 * DEEPNORM_BETA),
        'ln2_g': 1.0 + nrm(ks[21], (DEPTH, D_MODEL), 0.01),
        'ln2_b': nrm(ks[22], (DEPTH, D_MODEL), 0.01),
    }


def reference(x_prompt, x_sample, cache_kv_w128, cache_kv_w512, cache_kv_w2048, state_conv, state_ssm,
              w_in, conv_w, conv_b, dt_bias, a_log, d_skip, ssm_norm_g, w_ssm_br, w_attn_br, w_o,
              ln1_g, ln1_b, w_up, w_down, ln2_g, ln2_b):
    kv_caches = (cache_kv_w128, cache_kv_w512, cache_kv_w2048)
    xp, xs = x_prompt, x_sample
    p_kv = ([], [], [])
    s_kv = ([], [], [])
    p_conv, p_ssm, s_conv, s_ssm = [], [], [], []
    for l in range(DEPTH):
        params = (w_in[l], conv_w[l], conv_b[l], dt_bias[l], a_log[l], d_skip[l], ssm_norm_g[l],
                  w_ssm_br[l], w_attn_br[l], w_o[l], ln1_g[l], ln1_b[l], w_up[l], w_down[l],
                  ln2_g[l], ln2_b[l])
        xp, kv_rows, conv_new, h_new = decoder_layer(xp, 0, None, None, None, *params)
        for j in range(len(DIL_PATTERNS)):
            p_kv[j].append(kv_rows[j])
        p_conv.append(conv_new)
        p_ssm.append(h_new.astype(xp.dtype))
        bufs = (kv_caches[0][l], kv_caches[1][l], kv_caches[2][l])
        xs, kv_rows, conv_new, h_new = decoder_layer(xs, PAST_LEN, bufs, state_conv[l], state_ssm[l], *params)
        for j in range(len(DIL_PATTERNS)):
            s_kv[j].append(kv_rows[j])
        s_conv.append(conv_new)
        s_ssm.append(h_new.astype(state_ssm.dtype))
    return (xp, xs,
            jnp.stack(p_kv[0]), jnp.stack(p_kv[1]), jnp.stack(p_kv[2]), jnp.stack(p_conv), jnp.stack(p_ssm),
            jnp.stack(s_kv[0]), jnp.stack(s_kv[1]), jnp.stack(s_kv[2]), jnp.stack(s_conv), jnp.stack(s_ssm))
```

```python
import functools
import math

import jax
import jax.numpy as jnp
from jax import lax
from jax.experimental import pallas as pl
from jax.experimental.pallas import tpu as pltpu

F32 = jnp.float32
BF16 = jnp.bfloat16

SSM_HEAD_DIM = 64
SSM_GROUPS = 8
SSM_STATE = 128
CONV_WIDTH = 4
SSD_CHUNK = 128
ATTN_HEAD_DIM = 64
HEADS_PER_GROUP = 4
GROUP_WIDTH = HEADS_PER_GROUP * ATTN_HEAD_DIM
DIL_PATTERNS = ((128, 1), (512, 4), (2048, 16))
ROPE_DIM = 16
ROPE_THETA = 500000.0
LN_EPS = 1e-5
RMS_EPS = 1e-5
PAST_LEN = 2048
NEG_BIG = -1e30

LANES = 128
V7X_VMEM_LIMIT = 56 * 1024 * 1024


def _params(n_axes):
    return pltpu.CompilerParams(dimension_semantics=("arbitrary",) * n_axes,
                                vmem_limit_bytes=V7X_VMEM_LIMIT)


def _dot(a, b):
    return jnp.dot(a, b, preferred_element_type=F32)


def _dot_nt(a, b):
    return lax.dot_general(a, b, (((1,), (1,)), ((), ())), preferred_element_type=F32)


def _dot_tn(a, b):
    return lax.dot_general(a, b, (((0,), (0,)), ((), ())), preferred_element_type=F32)


def _split3(v):
    hi = v.astype(BF16)
    r1 = v - hi.astype(F32)
    mid = r1.astype(BF16)
    lo = (r1 - mid.astype(F32)).astype(BF16)
    return hi, mid, lo


def _expand_exact(v, e):
    hi, mid, lo = _split3(v)
    return _dot(hi, e) + _dot(mid, e) + _dot(lo, e)


def _stack_rows(rows, n):
    w = rows[0].shape[1]
    ri = lax.broadcasted_iota(jnp.int32, (n, w), 0)
    out = jnp.zeros((n, w), F32)
    for k, row in enumerate(rows):
        out = jnp.where(ri == k, row, out)
    return out


def _silu(v):
    return v * jax.nn.sigmoid(v)


def _layer_norm(r, g, b):
    mu = jnp.mean(r, axis=-1, keepdims=True)
    cen = r - mu
    var = jnp.mean(cen * cen, axis=-1, keepdims=True)
    return cen * lax.rsqrt(var + LN_EPS) * g + b


def _gated_group_norm(y, z, ng, out_ref, lead):
    hgt = y * _silu(z)
    gw = hgt.shape[1] // SSM_GROUPS
    for g in range(SSM_GROUPS):
        seg = hgt[:, g * gw:(g + 1) * gw]
        ms = jnp.mean(seg * seg, axis=-1, keepdims=True)
        val = seg * lax.rsqrt(ms + RMS_EPS) * ng[:, g * gw:(g + 1) * gw]
        out_ref[lead + (slice(None), slice(g * gw, (g + 1) * gw))] = val.astype(out_ref.dtype)


def _proj_a_kernel(x_ref, w_ref, o_ref, *, gate_tile):
    acc = _dot(x_ref[...], w_ref[...])
    j = pl.program_id(0)

    @pl.when(j == gate_tile)
    def _():
        o_ref[...] = jax.nn.sigmoid(acc)

    @pl.when(j != gate_tile)
    def _():
        o_ref[...] = acc


def _proj_a(x_bf, w_a, layer, tm):
    t, d = x_bf.shape
    n = w_a.shape[2]
    tn = 2048
    return pl.pallas_call(
        functools.partial(_proj_a_kernel, gate_tile=n // tn - 1),
        grid=(n // tn, t // tm),
        in_specs=[pl.BlockSpec((tm, d), lambda j, i: (i, 0)),
                  pl.BlockSpec((None, d, tn), lambda j, i: (layer, 0, j))],
        out_specs=pl.BlockSpec((tm, tn), lambda j, i: (i, j)),
        out_shape=jax.ShapeDtypeStruct((t, n), F32),
        compiler_params=_params(2),
        name="proj_a",
    )(x_bf, w_a)


def _proj_b_kernel(x_ref, w_ref, c_ref, s1_ref, s2_ref, dtb_ref,
                   q0, q1, q2, kv0, kv1, kv2, dt_ref):
    acc = _dot(x_ref[...], w_ref[...])
    c = c_ref[...]
    s1 = s1_ref[...]
    s2 = s2_ref[...]

    def rot(v):
        return v * c + pltpu.roll(v, 8, 1) * s1 + pltpu.roll(v, LANES - 8, 1) * s2

    qs = (q0, q1, q2)
    kvs = (kv0, kv1, kv2)
    q_total = 3 * GROUP_WIDTH
    for j in range(3):
        for u in range(2):
            lo = j * GROUP_WIDTH + u * LANES
            qs[j][:, u * LANES:(u + 1) * LANES] = rot(acc[:, lo:lo + LANES])
        base = q_total + j * 2 * GROUP_WIDTH
        for u in range(2):
            lo = base + u * LANES
            kvs[j][:, u * LANES:(u + 1) * LANES] = rot(acc[:, lo:lo + LANES])
        kvs[j][:, GROUP_WIDTH:2 * GROUP_WIDTH] = acc[:, base + GROUP_WIDTH:base + 2 * GROUP_WIDTH]
    dt_lo = q_total + 3 * 2 * GROUP_WIDTH
    v = acc[:, dt_lo:dt_lo + LANES] + dtb_ref[...]
    dt_ref[...] = jnp.maximum(v, 0.0) + jnp.log1p(jnp.exp(-jnp.abs(v)))


def _proj_b(x_bf, w_b, dtb, rope, layer, tm):
    t, d = x_bf.shape
    n = w_b.shape[2]
    cos_t, sin1_t, sin2_t = rope
    npos = cos_t.shape[0] // tm
    tab = pl.BlockSpec((tm, LANES), lambda i: (i % npos, 0))
    q_shape = jax.ShapeDtypeStruct((t, GROUP_WIDTH), F32)
    kv_shape = jax.ShapeDtypeStruct((t, 2 * GROUP_WIDTH), F32)
    q_spec = pl.BlockSpec((tm, GROUP_WIDTH), lambda i: (i, 0))
    kv_spec = pl.BlockSpec((tm, 2 * GROUP_WIDTH), lambda i: (i, 0))
    return pl.pallas_call(
        _proj_b_kernel,
        grid=(t // tm,),
        in_specs=[pl.BlockSpec((tm, d), lambda i: (i, 0)),
                  pl.BlockSpec((None, d, n), lambda i: (layer, 0, 0)),
                  tab, tab, tab,
                  pl.BlockSpec((None, 1, LANES), lambda i: (layer, 0, 0))],
        out_specs=[q_spec, q_spec, q_spec, kv_spec, kv_spec, kv_spec,
                   pl.BlockSpec((tm, LANES), lambda i: (i, 0))],
        out_shape=[q_shape, q_shape, q_shape, kv_shape, kv_shape, kv_shape,
                   jax.ShapeDtypeStruct((t, LANES), F32)],
        compiler_params=_params(1),
        name="proj_b",
    )(x_bf, w_b, cos_t, sin1_t, sin2_t, dtb)


def _ssd_prompt_kernel(xbc_ref, z_ref, dt_ref, cw_ref, cb_ref, alog_ref, dsk_ref, ng_ref,
                       y_ref, tail_ref, h_ref, xp_ref, ht_ref, ybuf_ref, *, n_heads):
    q = SSD_CHUNK
    d_inner = n_heads * SSM_HEAD_DIM
    gn = SSM_GROUPS * SSM_STATE
    gw = d_inner // SSM_GROUPS
    heads_per_group = n_heads // SSM_GROUPS
    c = pl.program_id(1)
    nc = pl.num_programs(1)

    @pl.when(c == 0)
    def _():
        xp_ref[0:8, :] = jnp.zeros((8, xp_ref.shape[1]), F32)
        ht_ref[...] = jnp.zeros(ht_ref.shape, F32)

    x = xbc_ref[...]
    xp_ref[8:8 + q, :] = x
    cw = cw_ref[...]
    conv = cb_ref[...] + x * cw[CONV_WIDTH - 1:CONV_WIDTH, :]
    for k in range(1, CONV_WIDTH):
        conv = conv + xp_ref[8 - k:8 - k + q, :] * cw[CONV_WIDTH - 1 - k:CONV_WIDTH - k, :]
    xp_ref[0:8, :] = x[q - 8:q, :]
    act = _silu(conv)
    xs = act[:, :d_inner]
    bm = act[:, d_inner:d_inner + gn]
    cm = act[:, d_inner + gn:]

    lane = lax.broadcasted_iota(jnp.int32, (q, LANES), 1)
    dt = jnp.where(lane < n_heads, dt_ref[...], 0.0)
    a = -jnp.exp(alog_ref[...])
    da = dt * a
    ri = lax.broadcasted_iota(jnp.int32, (q, q), 0)
    ci = lax.broadcasted_iota(jnp.int32, (q, q), 1)
    tril = ri >= ci
    acum = lax.dot_general(tril.astype(F32), da, (((1,), (0,)), ((), ())),
                           precision=lax.Precision.HIGHEST, preferred_element_type=F32)
    acum_t = acum.T
    dt_t = dt.T
    coef_t = dt_t * jnp.exp(acum_t[:, q - 1:q] - acum_t)
    cd = jnp.exp(acum[q - 1:q, :])

    glane = lax.broadcasted_iota(jnp.int32, (1, gw), 1)
    hmask = [(glane // SSM_HEAD_DIM) == r for r in range(heads_per_group)]

    for g in range(SSM_GROUPS):
        cg = cm[:, g * SSM_STATE:(g + 1) * SSM_STATE]
        bg = bm[:, g * SSM_STATE:(g + 1) * SSM_STATE]
        bg_t = bg.T
        cb = _dot(cg.astype(BF16), bg_t.astype(BF16))
        xg = xs[:, g * gw:(g + 1) * gw]
        htg = ht_ref[g]
        rhs_full = jnp.concatenate([xg, htg], axis=0)
        y_g = jnp.zeros((q, gw), F32)
        st_g = jnp.zeros((SSM_STATE, gw), F32)
        cd_g = jnp.zeros((1, gw), F32)
        for r in range(heads_per_group):
            h = g * heads_per_group + r
            acol = jnp.broadcast_to(acum[:, h:h + 1], (q, q))
            seg = acol - acum_t[h:h + 1, :]
            dec = jnp.exp(jnp.where(tril, seg, -jnp.inf))
            m = cb * dec * dt_t[h:h + 1, :]
            eac = jnp.exp(acol) * cg
            lhs = jnp.concatenate([m, eac], axis=1).astype(BF16)
            rhs = jnp.where(hmask[r], rhs_full, 0.0).astype(BF16)
            y_g = y_g + _dot(lhs, rhs)
            bs = (bg_t * coef_t[h:h + 1, :]).astype(BF16)
            st_g = st_g + _dot(bs, jnp.where(hmask[r], xg, 0.0).astype(BF16))
            cd_g = jnp.where(hmask[r], jnp.broadcast_to(cd[:, h:h + 1], (1, gw)), cd_g)
        ht_ref[g] = htg * cd_g + st_g
        ybuf_ref[:, g * gw:(g + 1) * gw] = y_g

    y = ybuf_ref[...] + dsk_ref[...] * xs
    _gated_group_norm(y, z_ref[...], ng_ref[...], y_ref, ())

    @pl.when(c == nc - 1)
    def _():
        tail_ref[0] = x[q - 8:q, :]
        pairs_per_group = heads_per_group // 2
        for g in range(SSM_GROUPS):
            for pr in range(pairs_per_group):
                blk = ht_ref[g, :, pr * LANES:(pr + 1) * LANES].T
                h0 = g * heads_per_group + 2 * pr
                h_ref[0, h0:h0 + 2] = blk.reshape(2, SSM_HEAD_DIM, SSM_STATE)


def _ssd_prompt(a_out, dt, conv_w, conv_b, alog_pad, dsk_x, ssm_norm_g, layer, bsz, seqlen):
    t = a_out.shape[0]
    conv_dim = conv_w.shape[2]
    d_inner = ssm_norm_g.shape[2]
    n_heads = d_inner // SSM_HEAD_DIM
    q = SSD_CHUNK
    nc = seqlen // q
    row = lambda b, c: (b * nc + c, 0)
    par = lambda b, c: (layer, 0, 0)
    return pl.pallas_call(
        functools.partial(_ssd_prompt_kernel, n_heads=n_heads),
        grid=(bsz, nc),
        in_specs=[pl.BlockSpec((q, conv_dim), row),
                  pl.BlockSpec((q, d_inner), lambda b, c: (b * nc + c, conv_dim // d_inner)),
                  pl.BlockSpec((q, LANES), row),
                  pl.BlockSpec((None, CONV_WIDTH, conv_dim), par),
                  pl.BlockSpec((None, 1, conv_dim), par),
                  pl.BlockSpec((None, 1, LANES), par),
                  pl.BlockSpec((None, 1, d_inner), par),
                  pl.BlockSpec((None, 1, d_inner), par)],
        out_specs=[pl.BlockSpec((q, d_inner), row),
                   pl.BlockSpec((1, 8, conv_dim), lambda b, c: (b, 0, 0)),
                   pl.BlockSpec((1, n_heads, SSM_HEAD_DIM, SSM_STATE), lambda b, c: (b, 0, 0, 0))],
        out_shape=[jax.ShapeDtypeStruct((t, d_inner), BF16),
                   jax.ShapeDtypeStruct((bsz, 8, conv_dim), F32),
                   jax.ShapeDtypeStruct((bsz, n_heads, SSM_HEAD_DIM, SSM_STATE), F32)],
        scratch_shapes=[pltpu.VMEM((8 + q, conv_dim), F32),
                        pltpu.VMEM((SSM_GROUPS, SSM_STATE, d_inner // SSM_GROUPS), F32),
                        pltpu.VMEM((q, d_inner), F32)],
        compiler_params=_params(2),
        name="ssd_prompt",
    )(a_out, a_out, dt, conv_w, conv_b, alog_pad, dsk_x, ssm_norm_g)


def _ssd_sample_kernel(xbc_ref, z_ref, dt_ref, sc_ref, h0_ref, cw_ref, cb_ref, alog_ref, dsk_ref, ng_ref,
                       e32_ref, eg_ref, y_ref, h_ref, *, n_heads, seqlen):
    d_inner = n_heads * SSM_HEAD_DIM
    gn = SSM_GROUPS * SSM_STATE
    gw = d_inner // SSM_GROUPS
    sc = sc_ref[0]
    x = xbc_ref[0]
    xp = [sc[s:s + 1, :] for s in range(CONV_WIDTH - 1)] + [x[s:s + 1, :] for s in range(seqlen)]
    cw = cw_ref[...]
    cb = cb_ref[...]
    xs, bm, cm = [], [], []
    for t in range(seqlen):
        acc = cb
        for tap in range(CONV_WIDTH):
            acc = acc + xp[t + tap] * cw[tap:tap + 1, :]
        act = _silu(acc)
        xs.append(act[:, :d_inner])
        bm.append(act[:, d_inner:d_inner + gn])
        cm.append(act[:, d_inner + gn:])

    lane = lax.broadcasted_iota(jnp.int32, (1, LANES), 1)
    dtv = dt_ref[0]
    dt = [jnp.where(lane < n_heads, dtv[t:t + 1, :], 0.0) for t in range(seqlen)]
    a = -jnp.exp(alog_ref[...])
    acum = []
    run = jnp.zeros((1, LANES), F32)
    for t in range(seqlen):
        run = run + dt[t] * a
        acum.append(run)
    last = seqlen - 1

    pairs = [(t, j) for t in range(seqlen) for j in range(t + 1)]
    rows = [jnp.exp(acum[t] - acum[j]) * dt[j] for (t, j) in pairs]
    ea_row0 = len(rows)
    rows += [jnp.exp(acum[t]) for t in range(seqlen)]
    ce_row0 = len(rows)
    rows += [jnp.exp(acum[last] - acum[j]) * dt[j] for j in range(seqlen)]
    n_rows = -(-len(rows) // 8) * 8
    rx = _expand_exact(_stack_rows(rows, n_rows), e32_ref[...])

    n_prod = -(-len(pairs) // 8) * 8
    prods = _stack_rows([cm[t] * bm[j] for (t, j) in pairs], n_prod)
    p_hi = prods.astype(BF16)
    p_lo = (prods - p_hi.astype(F32)).astype(BF16)
    eg = eg_ref[...]
    cbx = _dot(p_hi, eg) + _dot(p_lo, eg)

    crow = _stack_rows(cm, 8).astype(BF16)
    n_pairs = n_heads // 2
    heads_per_group = n_heads // SSM_GROUPS
    yoff_parts = []
    for k in range(n_pairs):
        g = (2 * k) // heads_per_group
        hp = h0_ref[0, 2 * k:2 * k + 2].reshape(2 * SSM_HEAD_DIM, SSM_STATE)
        yoff_parts.append(_dot_nt(crow[:, g * SSM_STATE:(g + 1) * SSM_STATE], hp.astype(BF16)))
    yoff = jnp.concatenate(yoff_parts, axis=1)

    dsk = dsk_ref[...]
    ys = []
    for t in range(seqlen):
        yt = rx[ea_row0 + t:ea_row0 + t + 1, :] * yoff[t:t + 1, :] + dsk * xs[t]
        for idx, (tt, j) in enumerate(pairs):
            if tt == t:
                yt = yt + cbx[idx:idx + 1, :] * rx[idx:idx + 1, :] * xs[j]
        ys.append(yt)
    y = _stack_rows(ys, seqlen)
    _gated_group_norm(y, z_ref[0], ng_ref[...], y_ref, (0,))

    cdx = rx[ea_row0 + last:ea_row0 + last + 1, :]
    cd_hi = cdx.astype(BF16).astype(F32)
    cd_r = cdx - cd_hi
    cd_mid = cd_r.astype(BF16).astype(F32)
    cd_lo = cd_r - cd_mid
    lhs_rows = [rx[ce_row0 + j:ce_row0 + j + 1, :] * xs[j] for j in range(seqlen)] + [cd_hi, cd_mid, cd_lo]
    lhs = _stack_rows(lhs_rows, 8).astype(BF16)
    zero_n = jnp.zeros((1, SSM_STATE), F32)
    one_n = jnp.ones((1, SSM_STATE), F32)
    rhs_by_group = []
    for g in range(SSM_GROUPS):
        rws = [jnp.concatenate([bm[j][:, g * SSM_STATE:(g + 1) * SSM_STATE], zero_n], axis=1)
               for j in range(seqlen)]
        rws += [jnp.concatenate([zero_n, one_n], axis=1)] * 3
        rhs_by_group.append(_stack_rows(rws, 8).astype(BF16))
    for k in range(n_pairs):
        g = (2 * k) // heads_per_group
        out = _dot_tn(lhs[:, k * LANES:(k + 1) * LANES], rhs_by_group[g])
        hp = h0_ref[0, 2 * k:2 * k + 2].reshape(2 * SSM_HEAD_DIM, SSM_STATE)
        hn = out[:, SSM_STATE:] * hp + out[:, :SSM_STATE]
        h_ref[0, 2 * k:2 * k + 2] = hn.reshape(2, SSM_HEAD_DIM, SSM_STATE)


def _ssd_sample(a_out, dt, state_conv, state_ssm, conv_w, conv_b, alog_pad, dsk_x, ssm_norm_g,
                e32, eg, layer, bsz, seqlen):
    conv_dim = conv_w.shape[2]
    d_inner = ssm_norm_g.shape[2]
    n_heads = d_inner // SSM_HEAD_DIM
    a3 = a_out.reshape(bsz, seqlen, a_out.shape[1])
    dt3 = dt.reshape(bsz, seqlen, LANES)
    par = lambda b: (layer, 0, 0)
    return pl.pallas_call(
        functools.partial(_ssd_sample_kernel, n_heads=n_heads, seqlen=seqlen),
        grid=(bsz,),
        in_specs=[pl.BlockSpec((1, seqlen, conv_dim), lambda b: (b, 0, 0)),
                  pl.BlockSpec((1, seqlen, d_inner), lambda b: (b, 0, conv_dim // d_inner)),
                  pl.BlockSpec((1, seqlen, LANES), lambda b: (b, 0, 0)),
                  pl.BlockSpec((None, 1, CONV_WIDTH - 1, conv_dim), lambda b: (layer, b, 0, 0)),
                  pl.BlockSpec((None, 1, n_heads, SSM_HEAD_DIM, SSM_STATE), lambda b: (layer, b, 0, 0, 0)),
                  pl.BlockSpec((None, CONV_WIDTH, conv_dim), par),
                  pl.BlockSpec((None, 1, conv_dim), par),
                  pl.BlockSpec((None, 1, LANES), par),
                  pl.BlockSpec((None, 1, d_inner), par),
                  pl.BlockSpec((None, 1, d_inner), par),
                  pl.BlockSpec(e32.shape, lambda b: (0, 0)),
                  pl.BlockSpec(eg.shape, lambda b: (0, 0))],
        out_specs=[pl.BlockSpec((1, seqlen, d_inner), lambda b: (b, 0, 0)),
                   pl.BlockSpec((1, n_heads, SSM_HEAD_DIM, SSM_STATE), lambda b: (b, 0, 0, 0))],
        out_shape=[jax.ShapeDtypeStruct((bsz, seqlen, d_inner), F32),
                   jax.ShapeDtypeStruct((bsz, n_heads, SSM_HEAD_DIM, SSM_STATE), F32)],
        compiler_params=_params(1),
        name="ssd_sample",
    )(a3, a3, dt3, state_conv, state_ssm, conv_w, conv_b, alog_pad, dsk_x, ssm_norm_g, e32, eg)


def _attn_prompt_kernel(q_ref, kv_ref, o_ref, lse_ref, *, dil, sub_len, win):
    tq = 128
    gw = GROUP_WIDTH
    scale = 1.0 / math.sqrt(ATTN_HEAD_DIM)
    lane = lax.broadcasted_iota(jnp.int32, (1, gw), 1)
    hmask = [(lane // ATTN_HEAD_DIM) == h for h in range(HEADS_PER_GROUP)]
    n_tiles = sub_len // tq

    for r in range(dil):
        def tile(ti, carry, r=r):
            if n_tiles == 1:
                m0 = 0
                ks = 0
            else:
                m0 = pl.multiple_of(ti * tq, tq)
                ks = pl.multiple_of(jnp.maximum(m0 - (win - tq), 0), tq)
            q = q_ref[0, pl.ds(m0, tq), r * gw:(r + 1) * gw]
            k = kv_ref[0, pl.ds(ks, win), 2 * r * gw:(2 * r + 1) * gw].astype(BF16)
            v = kv_ref[0, pl.ds(ks, win), (2 * r + 1) * gw:(2 * r + 2) * gw]
            qpos = m0 + lax.broadcasted_iota(jnp.int32, (tq, win), 0)
            kpos = ks + lax.broadcasted_iota(jnp.int32, (tq, win), 1)
            valid = (kpos <= qpos) & (kpos >= qpos - 128)
            o_acc = jnp.zeros((tq, gw), F32)
            lse_acc = jnp.zeros((tq, gw), F32)
            for h in range(HEADS_PER_GROUP):
                qh = jnp.where(hmask[h], q, 0.0).astype(BF16)
                s = _dot_nt(qh, k) * scale
                s = jnp.where(valid, s, NEG_BIG)
                m = jnp.max(s, axis=1, keepdims=True)
                p = jnp.exp(s - m)
                l = jnp.sum(p, axis=1, keepdims=True)
                pn = (p / l).astype(BF16)
                vh = jnp.where(hmask[h], v, 0.0).astype(BF16)
                o_acc = o_acc + _dot(pn, vh)
                lse_acc = jnp.where(hmask[h], m + jnp.log(l), lse_acc)
            o_ref[0, pl.ds(m0, tq), r * gw:(r + 1) * gw] = o_acc
            lse_ref[0, pl.ds(m0, tq), r * gw:(r + 1) * gw] = lse_acc
            return carry

        if n_tiles == 1:
            tile(0, 0)
        else:
            lax.fori_loop(0, n_tiles, tile, 0)


def _attn_prompt(q, kv, bsz, seqlen, dil):
    gw = GROUP_WIDTH
    sub_len = seqlen // dil
    win = min(256, sub_len)
    qv = q.reshape(bsz, sub_len, dil * gw)
    kvv = kv.reshape(bsz, sub_len, dil * 2 * gw)
    blk_q = pl.BlockSpec((1, sub_len, dil * gw), lambda b: (b, 0, 0))
    blk_kv = pl.BlockSpec((1, sub_len, dil * 2 * gw), lambda b: (b, 0, 0))
    shape = jax.ShapeDtypeStruct((bsz, sub_len, dil * gw), F32)
    o, lse = pl.pallas_call(
        functools.partial(_attn_prompt_kernel, dil=dil, sub_len=sub_len, win=win),
        grid=(bsz,),
        in_specs=[blk_q, blk_kv],
        out_specs=[blk_q, blk_q],
        out_shape=[shape, shape],
        compiler_params=_params(1),
        name=f"attn_prompt_d{dil}",
    )(qv, kvv)
    return o.reshape(bsz * seqlen, gw), lse.reshape(bsz * seqlen, gw)


def _attn_sample_kernel(q0, q1, q2, kn0, kn1, kn2, c0, c1, c2, ef_ref,
                        o0, o1, o2, l0, l1, l2, *, seqlen):
    gw = GROUP_WIDTH
    scale = 1.0 / math.sqrt(ATTN_HEAD_DIM)
    ef = ef_ref[...]
    qs, kns, caches, os_, ls = (q0, q1, q2), (kn0, kn1, kn2), (c0, c1, c2), (o0, o1, o2), (l0, l1, l2)
    for j, (_, dil) in enumerate(DIL_PATTERNS):
        q = qs[j][0]
        kvn = kns[j][0]
        kn = kvn[:, :gw]
        vn = kvn[:, gw:]
        cache = caches[j]
        n_rows = cache.shape[1]
        col = [0 if dil == 1 else 2 * gw * i for i in range(seqlen)]
        prods = [cache[0, :, col[i]:col[i] + gw] * q[i:i + 1, :] for i in range(seqlen)]
        s_all = _dot(jnp.concatenate(prods, axis=0).astype(BF16), ef) * scale
        new_pairs = [(i, t) for i in range(seqlen) for t in range(seqlen)
                     if (t <= i if dil == 1 else t == i)]
        n_new = -(-len(new_pairs) // 8) * 8
        pn = _stack_rows([q[i:i + 1, :] * kn[t:t + 1, :] for (i, t) in new_pairs], n_new)
        s_new = _dot(pn.astype(BF16), ef) * scale
        rowi = lax.broadcasted_iota(jnp.int32, (n_rows, gw), 0)
        o_rows, l_rows = [], []
        for i in range(seqlen):
            s = s_all[i * n_rows:(i + 1) * n_rows, :]
            if dil == 1:
                s = jnp.where(rowi >= i, s, NEG_BIG)
            m = jnp.max(s, axis=0, keepdims=True)
            mine = [idx for idx, (ii, _) in enumerate(new_pairs) if ii == i]
            for idx in mine:
                m = jnp.maximum(m, s_new[idx:idx + 1, :])
            p = jnp.exp(s - m)
            l = jnp.sum(p, axis=0, keepdims=True)
            vc = cache[0, :, col[i] + gw:col[i] + 2 * gw]
            acc = jnp.sum(p * vc, axis=0, keepdims=True)
            for idx in mine:
                t = new_pairs[idx][1]
                e = jnp.exp(s_new[idx:idx + 1, :] - m)
                l = l + e
                acc = acc + e * vn[t:t + 1, :]
            o_rows.append(acc / l)
            l_rows.append(m + jnp.log(l))
        os_[j][0] = _stack_rows(o_rows, seqlen)
        ls[j][0] = _stack_rows(l_rows, seqlen)


def _attn_sample(qs, kvs, caches, efull, layer, bsz, seqlen):
    gw = GROUP_WIDTH
    q3 = [q.reshape(bsz, seqlen, gw) for q in qs]
    kv3 = [kv.reshape(bsz, seqlen, 2 * gw) for kv in kvs]
    views, cache_specs = [], []
    for (window, dil), cache in zip(DIL_PATTERNS, caches):
        depth, b, rows = cache.shape[:3]
        view = cache.reshape(depth, b, rows // dil, dil * 2 * gw)
        views.append(view)
        width = min(dil, seqlen) * 2 * gw
        cache_specs.append(pl.BlockSpec((None, 1, rows // dil, width), lambda i: (layer, i, 0, 0)))
    q_spec = pl.BlockSpec((1, seqlen, gw), lambda i: (i, 0, 0))
    kv_spec = pl.BlockSpec((1, seqlen, 2 * gw), lambda i: (i, 0, 0))
    shape = jax.ShapeDtypeStruct((bsz, seqlen, gw), F32)
    outs = pl.pallas_call(
        functools.partial(_attn_sample_kernel, seqlen=seqlen),
        grid=(bsz,),
        in_specs=[q_spec] * 3 + [kv_spec] * 3 + cache_specs + [pl.BlockSpec(efull.shape, lambda i: (0, 0))],
        out_specs=[q_spec] * 6,
        out_shape=[shape] * 6,
        compiler_params=_params(1),
        name="attn_sample",
    )(*q3, *kv3, *views, efull)
    t = bsz * seqlen
    return [o.reshape(t, gw) for o in outs[:3]], [l.reshape(t, gw) for l in outs[3:]]


def _merge_kernel(y_ref, o0, o1, o2, l0, l1, l2, g_ref, x_ref, wssm_ref, wattn_ref, wo_ref,
                  lg_ref, lb_ref, xo_ref, xob_ref, *, alpha):
    d = x_ref.shape[1]
    l0v, l1v, l2v = l0[...], l1[...], l2[...]
    m = jnp.maximum(jnp.maximum(l0v, l1v), l2v)
    e0 = jnp.exp(l0v - m)
    e1 = jnp.exp(l1v - m)
    e2 = jnp.exp(l2v - m)
    o = (e0 * o0[...] + e1 * o1[...] + e2 * o2[...]) / (e0 + e1 + e2)
    ys = _dot(y_ref[...].astype(BF16), wssm_ref[...])
    ya = _dot(o.astype(BF16), wattn_ref[...])
    g = g_ref[...]
    merged = g[:, :d] * ys + g[:, d:] * ya
    mix = _dot(merged.astype(BF16), wo_ref[...])
    out = _layer_norm(alpha * x_ref[...] + mix, lg_ref[...], lb_ref[...])
    xo_ref[...] = out
    xob_ref[...] = out.astype(BF16)


def _merge(y_ssm, os_, lses, a_out, x, w_ssm_br, w_attn_br, w_o, ln_g, ln_b, layer, tm, alpha):
    t, d = x.shape
    d_inner = w_ssm_br.shape[1]
    gw = GROUP_WIDTH
    gate_blk = (a_out.shape[1] - 2 * d) // (2 * d)
    row = lambda i: (i, 0)
    par = lambda i: (layer, 0, 0)
    o_spec = pl.BlockSpec((tm, gw), row)
    return pl.pallas_call(
        functools.partial(_merge_kernel, alpha=alpha),
        grid=(t // tm,),
        in_specs=[pl.BlockSpec((tm, d_inner), row)] + [o_spec] * 6 +
                 [pl.BlockSpec((tm, 2 * d), lambda i: (i, gate_blk)),
                  pl.BlockSpec((tm, d), row),
                  pl.BlockSpec((None, d_inner, d), par),
                  pl.BlockSpec((None, gw, d), par),
                  pl.BlockSpec((None, d, d), par),
                  pl.BlockSpec((None, 1, d), par),
                  pl.BlockSpec((None, 1, d), par)],
        out_specs=[pl.BlockSpec((tm, d), row), pl.BlockSpec((tm, d), row)],
        out_shape=[jax.ShapeDtypeStruct((t, d), F32), jax.ShapeDtypeStruct((t, d), BF16)],
        compiler_params=_params(1),
        name="merge",
    )(y_ssm, *os_, *lses, a_out, x, w_ssm_br, w_attn_br, w_o, ln_g, ln_b)


def _mlp_kernel(xb_ref, x_ref, wup_ref, wdn_ref, lg_ref, lb_ref, xo_ref, xob_ref, acc_ref, *, alpha):
    f = pl.program_id(1)
    nf = pl.num_programs(1)
    hid = _dot(xb_ref[...], wup_ref[...])
    hid = jnp.square(jnp.maximum(hid, 0.0))
    part = _dot(hid.astype(BF16), wdn_ref[...])

    @pl.when(f == 0)
    def _():
        acc_ref[...] = part

    @pl.when(f > 0)
    def _():
        acc_ref[...] += part

    @pl.when(f == nf - 1)
    def _():
        out = _layer_norm(alpha * x_ref[...] + acc_ref[...], lg_ref[...], lb_ref[...])
        xo_ref[...] = out
        xob_ref[...] = out.astype(BF16)


def _mlp(x_bf, x, w_up, w_down, ln_g, ln_b, layer, tm, alpha):
    t, d = x.shape
    d_ff = w_up.shape[2]
    tf = 1024
    return pl.pallas_call(
        functools.partial(_mlp_kernel, alpha=alpha),
        grid=(t // tm, d_ff // tf),
        in_specs=[pl.BlockSpec((tm, d), lambda i, f: (i, 0)),
                  pl.BlockSpec((tm, d), lambda i, f: (i, 0)),
                  pl.BlockSpec((None, d, tf), lambda i, f: (layer, 0, f)),
                  pl.BlockSpec((None, tf, d), lambda i, f: (layer, f, 0)),
                  pl.BlockSpec((None, 1, d), lambda i, f: (layer, 0, 0)),
                  pl.BlockSpec((None, 1, d), lambda i, f: (layer, 0, 0))],
        out_specs=[pl.BlockSpec((tm, d), lambda i, f: (i, 0)), pl.BlockSpec((tm, d), lambda i, f: (i, 0))],
        out_shape=[jax.ShapeDtypeStruct((t, d), F32), jax.ShapeDtypeStruct((t, d), BF16)],
        scratch_shapes=[pltpu.VMEM((tm, d), F32)],
        compiler_params=_params(2),
        name="mlp",
    )(x_bf, x, w_up, w_down, ln_g, ln_b)


def _rope_tables(pos):
    half = ROPE_DIM // 2
    inv_freq = jnp.power(ROPE_THETA, -jnp.arange(half, dtype=F32) / half)
    ang = pos.astype(F32)[:, None] * inv_freq[None, :]
    cos, sin = jnp.cos(ang), jnp.sin(ang)
    n = pos.shape[0]
    pad = jnp.zeros((n, ATTN_HEAD_DIM - ROPE_DIM), F32)
    zero = jnp.zeros((n, half), F32)
    c_head = jnp.concatenate([cos, cos, pad + 1.0], axis=1)
    s1_head = jnp.concatenate([zero, sin, pad], axis=1)
    s2_head = jnp.concatenate([-sin, zero, pad], axis=1)
    rep = LANES // ATTN_HEAD_DIM
    return tuple(jnp.tile(a, (1, rep)) for a in (c_head, s1_head, s2_head))


def kernel(x_prompt, x_sample, cache_kv_w128, cache_kv_w512, cache_kv_w2048, state_conv, state_ssm,
           w_in, conv_w, conv_b, dt_bias, a_log, d_skip, ssm_norm_g, w_ssm_br, w_attn_br, w_o,
           ln1_g, ln1_b, w_up, w_down, ln2_g, ln2_b):
    depth, d_model, _ = w_in.shape
    bsz, seqlen, _ = x_prompt.shape
    dec_b, dec_s, _ = x_sample.shape
    d_inner = w_ssm_br.shape[1]
    conv_dim = conv_w.shape[2]
    n_heads = dt_bias.shape[1]
    attn_w = 3 * GROUP_WIDTH
    gn = SSM_GROUPS * SSM_STATE
    alpha = (2 * depth) ** 0.25

    off_xbc = d_inner
    off_dt = off_xbc + conv_dim
    off_q = off_dt + n_heads
    off_k = off_q + attn_w
    off_v = off_k + attn_w
    off_g = off_v + attn_w

    w_a = jnp.concatenate([w_in[:, :, off_xbc:off_dt], w_in[:, :, :off_xbc], w_in[:, :, off_g:]],
                          axis=-1).astype(BF16)
    cols = [w_in[:, :, off_q + j * GROUP_WIDTH:off_q + (j + 1) * GROUP_WIDTH] for j in range(3)]
    for j in range(3):
        cols.append(w_in[:, :, off_k + j * GROUP_WIDTH:off_k + (j + 1) * GROUP_WIDTH])
        cols.append(w_in[:, :, off_v + j * GROUP_WIDTH:off_v + (j + 1) * GROUP_WIDTH])
    cols.append(jnp.pad(w_in[:, :, off_dt:off_q], ((0, 0), (0, 0), (0, LANES - n_heads))))
    w_b = jnp.concatenate(cols, axis=-1).astype(BF16)
    w_ssm_bf = w_ssm_br.astype(BF16)
    w_attn_bf = w_attn_br.astype(BF16)
    w_o_bf = w_o.astype(BF16)
    w_up_bf = w_up.astype(BF16)
    w_down_bf = w_down.astype(BF16)

    r3 = lambda a: a.reshape(depth, 1, a.shape[-1])
    dtb = r3(jnp.pad(dt_bias, ((0, 0), (0, LANES - n_heads))))
    alog_pad = r3(jnp.pad(a_log, ((0, 0), (0, LANES - n_heads))))
    dsk_x = r3(jnp.repeat(d_skip, SSM_HEAD_DIM, axis=1))
    conv_b3, ng3 = r3(conv_b), r3(ssm_norm_g)
    ln1_g3, ln1_b3, ln2_g3, ln2_b3 = r3(ln1_g), r3(ln1_b), r3(ln2_g), r3(ln2_b)

    e32 = (jnp.arange(LANES)[:, None] == jnp.arange(d_inner)[None, :] // SSM_HEAD_DIM).astype(BF16)
    eg = (jnp.arange(gn)[:, None] // SSM_STATE ==
          jnp.arange(d_inner)[None, :] // (d_inner // SSM_GROUPS)).astype(BF16)
    efull = (jnp.arange(GROUP_WIDTH)[:, None] // ATTN_HEAD_DIM ==
             jnp.arange(GROUP_WIDTH)[None, :] // ATTN_HEAD_DIM).astype(BF16)

    rope_p = _rope_tables(jnp.arange(seqlen))
    rope_s = _rope_tables(PAST_LEN + jnp.arange(dec_b * dec_s) % dec_s)

    tp = bsz * seqlen
    ts = dec_b * dec_s
    xp = x_prompt.reshape(tp, d_model)
    xs = x_sample.reshape(ts, d_model)
    xp_bf, xs_bf = xp.astype(BF16), xs.astype(BF16)
    caches = (cache_kv_w128, cache_kv_w512, cache_kv_w2048)

    p_kv = ([], [], [])
    s_kv = ([], [], [])
    p_conv, p_ssm, s_conv, s_ssm = [], [], [], []
    tm_p, tm_s = 1024, ts
    for l in range(depth):
        a_out = _proj_a(xp_bf, w_a, l, tm_p)
        q0, q1, q2, kv0, kv1, kv2, dt = _proj_b(xp_bf, w_b, dtb, rope_p, l, 512)
        y_ssm, tail, h_new = _ssd_prompt(a_out, dt, conv_w, conv_b3, alog_pad, dsk_x, ng3, l, bsz, seqlen)
        os_, lses = [], []
        for (window, dil), qj, kvj in zip(DIL_PATTERNS, (q0, q1, q2), (kv0, kv1, kv2)):
            o, lse = _attn_prompt(qj, kvj, bsz, seqlen, dil)
            os_.append(o)
            lses.append(lse)
        x1, x1_bf = _merge(y_ssm, os_, lses, a_out, xp, w_ssm_bf, w_attn_bf, w_o_bf, ln1_g3, ln1_b3,
                           l, 512, alpha)
        xp, xp_bf = _mlp(x1_bf, x1, w_up_bf, w_down_bf, ln2_g3, ln2_b3, l, tm_p, alpha)
        for j, ((window, _), kvj) in enumerate(zip(DIL_PATTERNS, (kv0, kv1, kv2))):
            rows = kvj.reshape(bsz, seqlen, 2, HEADS_PER_GROUP, ATTN_HEAD_DIM)
            p_kv[j].append(rows[:, seqlen - min(window, seqlen):])
        p_conv.append(tail[:, 8 - (CONV_WIDTH - 1):, :])
        p_ssm.append(h_new)

        a_out = _proj_a(xs_bf, w_a, l, tm_s)
        q0, q1, q2, kv0, kv1, kv2, dt = _proj_b(xs_bf, w_b, dtb, rope_s, l, tm_s)
        y_ssm, h_new = _ssd_sample(a_out, dt, state_conv, state_ssm, conv_w, conv_b3, alog_pad, dsk_x, ng3,
                                   e32, eg, l, dec_b, dec_s)
        os_, lses = _attn_sample((q0, q1, q2), (kv0, kv1, kv2), caches, efull, l, dec_b, dec_s)
        x1, x1_bf = _merge(y_ssm.reshape(ts, d_inner), os_, lses, a_out, xs, w_ssm_bf, w_attn_bf, w_o_bf,
                           ln1_g3, ln1_b3, l, tm_s, alpha)
        xs, xs_bf = _mlp(x1_bf, x1, w_up_bf, w_down_bf, ln2_g3, ln2_b3, l, tm_s, alpha)
        for j, kvj in enumerate((kv0, kv1, kv2)):
            s_kv[j].append(kvj.reshape(dec_b, dec_s, 2, HEADS_PER_GROUP, ATTN_HEAD_DIM))
        xbc_rows = a_out.reshape(dec_b, dec_s, a_out.shape[1])[:, :, :conv_dim]
        s_conv.append(xbc_rows[:, dec_s - (CONV_WIDTH - 1):])
        s_ssm.append(h_new)

    return (xp.reshape(bsz, seqlen, d_model), xs.reshape(dec_b, dec_s, d_model),
            jnp.stack(p_kv[0]), jnp.stack(p_kv[1]), jnp.stack(p_kv[2]), jnp.stack(p_conv), jnp.stack(p_ssm),
            jnp.stack(s_kv[0]), jnp.stack(s_kv[1]), jnp.stack(s_kv[2]), jnp.stack(s_conv), jnp.stack(s_ssm))
```

```python
import functools
import math

import jax
import jax.numpy as jnp
from jax import lax
from jax.experimental import pallas as pl
from jax.experimental.pallas import tpu as pltpu

F32 = jnp.float32
BF16 = jnp.bfloat16

SSM_HEAD_DIM = 64
SSM_GROUPS = 8
SSM_STATE = 128
CONV_WIDTH = 4
SSD_CHUNK = 128
CONV_SUB = 256
ATTN_HEAD_DIM = 64
HEADS_PER_GROUP = 4
GROUP_WIDTH = HEADS_PER_GROUP * ATTN_HEAD_DIM
DIL_PATTERNS = ((128, 1), (512, 4), (2048, 16))
ROPE_DIM = 16
ROPE_THETA = 500000.0
LN_EPS = 1e-5
RMS_EPS = 1e-5
PAST_LEN = 2048
NEG_BIG = -1e30

LANES = 128
SUBLANES = 8
V7X_VMEM_LIMIT = 56 * 1024 * 1024


def _params(n_axes):
    return pltpu.CompilerParams(dimension_semantics=("arbitrary",) * n_axes,
                                vmem_limit_bytes=V7X_VMEM_LIMIT)


def _dot(a, b):
    return jnp.dot(a, b, preferred_element_type=F32)


def _dot_nt(a, b):
    return lax.dot_general(a, b, (((1,), (1,)), ((), ())), preferred_element_type=F32)


def _dot_tn(a, b):
    return lax.dot_general(a, b, (((0,), (0,)), ((), ())), preferred_element_type=F32)


def _split3(v):
    hi = v.astype(BF16)
    r1 = v - hi.astype(F32)
    mid = r1.astype(BF16)
    lo = (r1 - mid.astype(F32)).astype(BF16)
    return hi, mid, lo


def _expand_exact(v, e):
    hi, mid, lo = _split3(v)
    return _dot(hi, e) + _dot(mid, e) + _dot(lo, e)


def _stack_rows(rows, n):
    w = rows[0].shape[1]
    ri = lax.broadcasted_iota(jnp.int32, (n, w), 0)
    out = jnp.zeros((n, w), F32)
    for k, row in enumerate(rows):
        out = jnp.where(ri == k, row, out)
    return out


def _silu(v):
    return v * jax.nn.sigmoid(v)


def _layer_norm(r, g, b):
    mu = jnp.mean(r, axis=-1, keepdims=True)
    cen = r - mu
    var = jnp.mean(cen * cen, axis=-1, keepdims=True)
    return cen * lax.rsqrt(var + LN_EPS) * g + b


def _gated_group_norm(y, zs, ng, out_ref, lead):
    hgt = y * zs
    gw = hgt.shape[1] // SSM_GROUPS
    for g in range(SSM_GROUPS):
        seg = hgt[:, g * gw:(g + 1) * gw]
        ms = jnp.mean(seg * seg, axis=-1, keepdims=True)
        val = seg * lax.rsqrt(ms + RMS_EPS) * ng[:, g * gw:(g + 1) * gw]
        out_ref[lead + (slice(None), slice(g * gw, (g + 1) * gw))] = val.astype(out_ref.dtype)


def _proj_plain_kernel(x_ref, w_ref, o_ref):
    o_ref[...] = _dot(x_ref[...], w_ref[...])


def _proj_zg_kernel(x_ref, w_ref, o_ref):
    acc = _dot(x_ref[...], w_ref[...])
    j = pl.program_id(0)

    @pl.when(j == 0)
    def _():
        o_ref[...] = _silu(acc).astype(o_ref.dtype)

    @pl.when(j != 0)
    def _():
        o_ref[...] = jax.nn.sigmoid(acc).astype(o_ref.dtype)


def _proj_a(x_bf, w, layer, tm, tn, body, out_dtype, name):
    t, d = x_bf.shape
    n = w.shape[2]
    return pl.pallas_call(
        body,
        grid=(n // tn, t // tm),
        in_specs=[pl.BlockSpec((tm, d), lambda j, i: (i, 0)),
                  pl.BlockSpec((None, d, tn), lambda j, i: (layer, 0, j))],
        out_specs=pl.BlockSpec((tm, tn), lambda j, i: (i, j)),
        out_shape=jax.ShapeDtypeStruct((t, n), out_dtype),
        compiler_params=_params(2),
        name=name,
    )(x_bf, w)


def _proj_xbc_conv_kernel(x_ref, w_ref, cw_ref, cb_ref, o_ref, tail_ref, carry_ref, *, tiles_per_seq):
    i = pl.program_id(1)
    tm, tn = o_ref.shape
    pos_in_seq = i % tiles_per_seq
    prev = jnp.where(pos_in_seq == 0, 0.0, carry_ref[...])
    cw = cw_ref[...]
    cb = cb_ref[...]
    w = w_ref[...]
    row8 = lax.broadcasted_iota(jnp.int32, (SUBLANES, tn), 0)
    for s in range(tm // CONV_SUB):
        raw = _dot(x_ref[s * CONV_SUB:(s + 1) * CONV_SUB, :], w)
        conv = cb + raw * cw[CONV_WIDTH - 1:CONV_WIDTH, :]
        for k in range(1, CONV_WIDTH):
            sh = pltpu.roll(raw, k, 0)
            top = jnp.where(row8 < k, pltpu.roll(prev, k, 0), sh[0:SUBLANES, :])
            sh = jnp.concatenate([top, sh[SUBLANES:, :]], axis=0)
            conv = conv + sh * cw[CONV_WIDTH - 1 - k:CONV_WIDTH - k, :]
        o_ref[s * CONV_SUB:(s + 1) * CONV_SUB, :] = _silu(conv).astype(o_ref.dtype)
        prev = raw[CONV_SUB - SUBLANES:CONV_SUB, :]
    carry_ref[...] = prev

    @pl.when(pos_in_seq == tiles_per_seq - 1)
    def _():
        tail_ref[0] = prev


def _proj_xbc_conv(x_bf, w, conv_w, conv_b, layer, tm, tn, bsz, seqlen):
    t, d = x_bf.shape
    n = w.shape[2]
    tiles_per_seq = seqlen // tm
    return pl.pallas_call(
        functools.partial(_proj_xbc_conv_kernel, tiles_per_seq=tiles_per_seq),
        grid=(n // tn, t // tm),
        in_specs=[pl.BlockSpec((tm, d), lambda j, i: (i, 0)),
                  pl.BlockSpec((None, d, tn), lambda j, i: (layer, 0, j)),
                  pl.BlockSpec((None, CONV_WIDTH, tn), lambda j, i: (layer, 0, j)),
                  pl.BlockSpec((None, 1, tn), lambda j, i: (layer, 0, j))],
        out_specs=[pl.BlockSpec((tm, tn), lambda j, i: (i, j)),
                   pl.BlockSpec((1, SUBLANES, tn), lambda j, i: (i // tiles_per_seq, 0, j))],
        out_shape=[jax.ShapeDtypeStruct((t, n), BF16),
                   jax.ShapeDtypeStruct((bsz, SUBLANES, n), F32)],
        scratch_shapes=[pltpu.VMEM((SUBLANES, tn), F32)],
        compiler_params=_params(2),
        name="proj_xbc_conv",
    )(x_bf, w, conv_w, conv_b)


def _proj_b_kernel(x_ref, w_ref, c_ref, s1_ref, s2_ref, dtb_ref,
                   q0, q1, q2, kv0, kv1, kv2, dt_ref):
    acc = _dot(x_ref[...], w_ref[...])
    c = c_ref[...]
    s1 = s1_ref[...]
    s2 = s2_ref[...]

    def rot(v):
        return v * c + pltpu.roll(v, 8, 1) * s1 + pltpu.roll(v, LANES - 8, 1) * s2

    qs = (q0, q1, q2)
    kvs = (kv0, kv1, kv2)
    q_total = 3 * GROUP_WIDTH
    for j in range(3):
        for u in range(2):
            lo = j * GROUP_WIDTH + u * LANES
            qs[j][:, u * LANES:(u + 1) * LANES] = rot(acc[:, lo:lo + LANES])
        base = q_total + j * 2 * GROUP_WIDTH
        for u in range(2):
            lo = base + u * LANES
            kvs[j][:, u * LANES:(u + 1) * LANES] = rot(acc[:, lo:lo + LANES])
        kvs[j][:, GROUP_WIDTH:2 * GROUP_WIDTH] = acc[:, base + GROUP_WIDTH:base + 2 * GROUP_WIDTH]
    dt_lo = q_total + 3 * 2 * GROUP_WIDTH
    v = acc[:, dt_lo:dt_lo + LANES] + dtb_ref[...]
    dt_ref[...] = jnp.maximum(v, 0.0) + jnp.log1p(jnp.exp(-jnp.abs(v)))


def _proj_b(x_bf, w_b, dtb, rope, layer, tm):
    t, d = x_bf.shape
    n = w_b.shape[2]
    cos_t, sin1_t, sin2_t = rope
    npos = cos_t.shape[0] // tm
    tab = pl.BlockSpec((tm, LANES), lambda i: (i % npos, 0))
    q_shape = jax.ShapeDtypeStruct((t, GROUP_WIDTH), F32)
    kv_shape = jax.ShapeDtypeStruct((t, 2 * GROUP_WIDTH), F32)
    q_spec = pl.BlockSpec((tm, GROUP_WIDTH), lambda i: (i, 0))
    kv_spec = pl.BlockSpec((tm, 2 * GROUP_WIDTH), lambda i: (i, 0))
    return pl.pallas_call(
        _proj_b_kernel,
        grid=(t // tm,),
        in_specs=[pl.BlockSpec((tm, d), lambda i: (i, 0)),
                  pl.BlockSpec((None, d, n), lambda i: (layer, 0, 0)),
                  tab, tab, tab,
                  pl.BlockSpec((None, 1, LANES), lambda i: (layer, 0, 0))],
        out_specs=[q_spec, q_spec, q_spec, kv_spec, kv_spec, kv_spec,
                   pl.BlockSpec((tm, LANES), lambda i: (i, 0))],
        out_shape=[q_shape, q_shape, q_shape, kv_shape, kv_shape, kv_shape,
                   jax.ShapeDtypeStruct((t, LANES), F32)],
        compiler_params=_params(1),
        name="proj_b",
    )(x_bf, w_b, cos_t, sin1_t, sin2_t, dtb)


def _ssd_prompt_kernel(xbc_ref, zs_ref, dt_ref, alog_ref, dsk_ref, ng_ref,
                       y_ref, h_ref, ht_ref, ybuf_ref, *, n_heads):
    q = SSD_CHUNK
    d_inner = n_heads * SSM_HEAD_DIM
    gn = SSM_GROUPS * SSM_STATE
    gw = d_inner // SSM_GROUPS
    heads_per_group = n_heads // SSM_GROUPS
    c = pl.program_id(1)
    nc = pl.num_programs(1)

    @pl.when(c == 0)
    def _():
        ht_ref[...] = jnp.zeros(ht_ref.shape, F32)

    lane = lax.broadcasted_iota(jnp.int32, (q, LANES), 1)
    dt = jnp.where(lane < n_heads, dt_ref[...], 0.0)
    a = -jnp.exp(alog_ref[...])
    da = dt * a
    ri = lax.broadcasted_iota(jnp.int32, (q, q), 0)
    ci = lax.broadcasted_iota(jnp.int32, (q, q), 1)
    tril = ri >= ci
    acum = lax.dot_general(tril.astype(F32), da, (((1,), (0,)), ((), ())),
                           precision=lax.Precision.HIGHEST, preferred_element_type=F32)
    acum_t = acum.T
    dt_t = dt.T
    coef_t = dt_t * jnp.exp(acum_t[:, q - 1:q] - acum_t)
    cd = jnp.exp(acum[q - 1:q, :])

    glane = lax.broadcasted_iota(jnp.int32, (1, gw), 1)
    hmask = [(glane // SSM_HEAD_DIM) == r for r in range(heads_per_group)]
    zero_bf = jnp.zeros((), BF16)

    for g in range(SSM_GROUPS):
        cg_bf = xbc_ref[:, d_inner + gn + g * SSM_STATE:d_inner + gn + (g + 1) * SSM_STATE]
        bg_bf = xbc_ref[:, d_inner + g * SSM_STATE:d_inner + (g + 1) * SSM_STATE]
        cg = cg_bf.astype(F32)
        bg_t = bg_bf.astype(F32).T
        cb = _dot_nt(cg_bf, bg_bf)
        xg = xbc_ref[:, g * gw:(g + 1) * gw]
        htg = ht_ref[g]
        rhs_full = jnp.concatenate([xg, htg.astype(BF16)], axis=0)
        y_g = jnp.zeros((q, gw), F32)
        st_g = jnp.zeros((SSM_STATE, gw), F32)
        cd_g = jnp.zeros((1, gw), F32)
        for r in range(heads_per_group):
            h = g * heads_per_group + r
            acol = jnp.broadcast_to(acum[:, h:h + 1], (q, q))
            seg = acol - acum_t[h:h + 1, :]
            dec = jnp.exp(jnp.where(tril, seg, -jnp.inf))
            m = cb * dec * dt_t[h:h + 1, :]
            eac = jnp.exp(acol) * cg
            lhs = jnp.concatenate([m, eac], axis=1).astype(BF16)
            rhs = jnp.where(hmask[r], rhs_full, zero_bf)
            y_g = y_g + _dot(lhs, rhs)
            bs = (bg_t * coef_t[h:h + 1, :]).astype(BF16)
            st_g = st_g + _dot(bs, jnp.where(hmask[r], xg, zero_bf))
            cd_g = jnp.where(hmask[r], jnp.broadcast_to(cd[:, h:h + 1], (1, gw)), cd_g)
        ht_ref[g] = htg * cd_g + st_g
        ybuf_ref[:, g * gw:(g + 1) * gw] = y_g

    y = ybuf_ref[...] + dsk_ref[...] * xbc_ref[:, :d_inner].astype(F32)
    _gated_group_norm(y, zs_ref[...].astype(F32), ng_ref[...], y_ref, ())

    @pl.when(c == nc - 1)
    def _():
        pairs_per_group = heads_per_group // 2
        for g in range(SSM_GROUPS):
            for pr in range(pairs_per_group):
                blk = ht_ref[g, :, pr * LANES:(pr + 1) * LANES].T
                h0 = g * heads_per_group + 2 * pr
                h_ref[0, h0:h0 + 2] = blk.reshape(2, SSM_HEAD_DIM, SSM_STATE)


def _ssd_prompt(xbc_act, zg, dt, alog_pad, dsk_x, ssm_norm_g, layer, bsz, seqlen):
    t, conv_dim = xbc_act.shape
    d_inner = ssm_norm_g.shape[2]
    n_heads = d_inner // SSM_HEAD_DIM
    q = SSD_CHUNK
    nc = seqlen // q
    row = lambda b, c: (b * nc + c, 0)
    par = lambda b, c: (layer, 0, 0)
    return pl.pallas_call(
        functools.partial(_ssd_prompt_kernel, n_heads=n_heads),
        grid=(bsz, nc),
        in_specs=[pl.BlockSpec((q, conv_dim), row),
                  pl.BlockSpec((q, d_inner), row),
                  pl.BlockSpec((q, LANES), row),
                  pl.BlockSpec((None, 1, LANES), par),
                  pl.BlockSpec((None, 1, d_inner), par),
                  pl.BlockSpec((None, 1, d_inner), par)],
        out_specs=[pl.BlockSpec((q, d_inner), row),
                   pl.BlockSpec((1, n_heads, SSM_HEAD_DIM, SSM_STATE), lambda b, c: (b, 0, 0, 0))],
        out_shape=[jax.ShapeDtypeStruct((t, d_inner), BF16),
                   jax.ShapeDtypeStruct((bsz, n_heads, SSM_HEAD_DIM, SSM_STATE), F32)],
        scratch_shapes=[pltpu.VMEM((SSM_GROUPS, SSM_STATE, d_inner // SSM_GROUPS), F32),
                        pltpu.VMEM((q, d_inner), F32)],
        compiler_params=_params(2),
        name="ssd_prompt",
    )(xbc_act, zg, dt, alog_pad, dsk_x, ssm_norm_g)


def _ssd_sample_kernel(xbc_ref, zs_ref, dt_ref, sc_ref, h0_ref, cw_ref, cb_ref, alog_ref, dsk_ref, ng_ref,
                       e32_ref, eg_ref, y_ref, h_ref, *, n_heads, seqlen):
    d_inner = n_heads * SSM_HEAD_DIM
    gn = SSM_GROUPS * SSM_STATE
    sc = sc_ref[0]
    x = xbc_ref[0]
    xp = [sc[s:s + 1, :] for s in range(CONV_WIDTH - 1)] + [x[s:s + 1, :] for s in range(seqlen)]
    cw = cw_ref[...]
    cb = cb_ref[...]
    xs, bm, cm = [], [], []
    for t in range(seqlen):
        acc = cb
        for tap in range(CONV_WIDTH):
            acc = acc + xp[t + tap] * cw[tap:tap + 1, :]
        act = _silu(acc)
        xs.append(act[:, :d_inner])
        bm.append(act[:, d_inner:d_inner + gn])
        cm.append(act[:, d_inner + gn:])

    lane = lax.broadcasted_iota(jnp.int32, (1, LANES), 1)
    dtv = dt_ref[0]
    dt = [jnp.where(lane < n_heads, dtv[t:t + 1, :], 0.0) for t in range(seqlen)]
    a = -jnp.exp(alog_ref[...])
    acum = []
    run = jnp.zeros((1, LANES), F32)
    for t in range(seqlen):
        run = run + dt[t] * a
        acum.append(run)
    last = seqlen - 1

    pairs = [(t, j) for t in range(seqlen) for j in range(t + 1)]
    rows = [jnp.exp(acum[t] - acum[j]) * dt[j] for (t, j) in pairs]
    ea_row0 = len(rows)
    rows += [jnp.exp(acum[t]) for t in range(seqlen)]
    ce_row0 = len(rows)
    rows += [jnp.exp(acum[last] - acum[j]) * dt[j] for j in range(seqlen)]
    n_rows = -(-len(rows) // 8) * 8
    rx = _expand_exact(_stack_rows(rows, n_rows), e32_ref[...])

    n_prod = -(-len(pairs) // 8) * 8
    prods = _stack_rows([cm[t] * bm[j] for (t, j) in pairs], n_prod)
    p_hi = prods.astype(BF16)
    p_lo = (prods - p_hi.astype(F32)).astype(BF16)
    eg = eg_ref[...]
    cbx = _dot(p_hi, eg) + _dot(p_lo, eg)

    crow = _stack_rows(cm, 8).astype(BF16)
    n_pairs = n_heads // 2
    heads_per_group = n_heads // SSM_GROUPS
    yoff_parts = []
    for k in range(n_pairs):
        g = (2 * k) // heads_per_group
        hp = h0_ref[0, 2 * k:2 * k + 2].reshape(2 * SSM_HEAD_DIM, SSM_STATE)
        yoff_parts.append(_dot_nt(crow[:, g * SSM_STATE:(g + 1) * SSM_STATE], hp.astype(BF16)))
    yoff = jnp.concatenate(yoff_parts, axis=1)

    dsk = dsk_ref[...]
    ys = []
    for t in range(seqlen):
        yt = rx[ea_row0 + t:ea_row0 + t + 1, :] * yoff[t:t + 1, :] + dsk * xs[t]
        for idx, (tt, j) in enumerate(pairs):
            if tt == t:
                yt = yt + cbx[idx:idx + 1, :] * rx[idx:idx + 1, :] * xs[j]
        ys.append(yt)
    y = _stack_rows(ys, seqlen)
    _gated_group_norm(y, zs_ref[0].astype(F32), ng_ref[...], y_ref, (0,))

    cdx = rx[ea_row0 + last:ea_row0 + last + 1, :]
    cd_hi = cdx.astype(BF16).astype(F32)
    cd_r = cdx - cd_hi
    cd_mid = cd_r.astype(BF16).astype(F32)
    cd_lo = cd_r - cd_mid
    lhs_rows = [rx[ce_row0 + j:ce_row0 + j + 1, :] * xs[j] for j in range(seqlen)] + [cd_hi, cd_mid, cd_lo]
    lhs = _stack_rows(lhs_rows, 8).astype(BF16)
    zero_n = jnp.zeros((1, SSM_STATE), F32)
    one_n = jnp.ones((1, SSM_STATE), F32)
    rhs_by_group = []
    for g in range(SSM_GROUPS):
        rws = [jnp.concatenate([bm[j][:, g * SSM_STATE:(g + 1) * SSM_STATE], zero_n], axis=1)
               for j in range(seqlen)]
        rws += [jnp.concatenate([zero_n, one_n], axis=1)] * 3
        rhs_by_group.append(_stack_rows(rws, 8).astype(BF16))
    for k in range(n_pairs):
        g = (2 * k) // heads_per_group
        out = _dot_tn(lhs[:, k * LANES:(k + 1) * LANES], rhs_by_group[g])
        hp = h0_ref[0, 2 * k:2 * k + 2].reshape(2 * SSM_HEAD_DIM, SSM_STATE)
        hn = out[:, SSM_STATE:] * hp + out[:, :SSM_STATE]
        h_ref[0, 2 * k:2 * k + 2] = hn.reshape(2, SSM_HEAD_DIM, SSM_STATE)


def _ssd_sample(xbc_raw, zg, dt, state_conv, state_ssm, conv_w, conv_b, alog_pad, dsk_x, ssm_norm_g,
                e32, eg, layer, bsz, seqlen):
    conv_dim = conv_w.shape[2]
    d_inner = ssm_norm_g.shape[2]
    n_heads = d_inner // SSM_HEAD_DIM
    x3 = xbc_raw.reshape(bsz, seqlen, conv_dim)
    zg3 = zg.reshape(bsz, seqlen, zg.shape[1])
    dt3 = dt.reshape(bsz, seqlen, LANES)
    par = lambda b: (layer, 0, 0)
    return pl.pallas_call(
        functools.partial(_ssd_sample_kernel, n_heads=n_heads, seqlen=seqlen),
        grid=(bsz,),
        in_specs=[pl.BlockSpec((1, seqlen, conv_dim), lambda b: (b, 0, 0)),
                  pl.BlockSpec((1, seqlen, d_inner), lambda b: (b, 0, 0)),
                  pl.BlockSpec((1, seqlen, LANES), lambda b: (b, 0, 0)),
                  pl.BlockSpec((None, 1, CONV_WIDTH - 1, conv_dim), lambda b: (layer, b, 0, 0)),
                  pl.BlockSpec((None, 1, n_heads, SSM_HEAD_DIM, SSM_STATE), lambda b: (layer, b, 0, 0, 0)),
                  pl.BlockSpec((None, CONV_WIDTH, conv_dim), par),
                  pl.BlockSpec((None, 1, conv_dim), par),
                  pl.BlockSpec((None, 1, LANES), par),
                  pl.BlockSpec((None, 1, d_inner), par),
                  pl.BlockSpec((None, 1, d_inner), par),
                  pl.BlockSpec(e32.shape, lambda b: (0, 0)),
                  pl.BlockSpec(eg.shape, lambda b: (0, 0))],
        out_specs=[pl.BlockSpec((1, seqlen, d_inner), lambda b: (b, 0, 0)),
                   pl.BlockSpec((1, n_heads, SSM_HEAD_DIM, SSM_STATE), lambda b: (b, 0, 0, 0))],
        out_shape=[jax.ShapeDtypeStruct((bsz, seqlen, d_inner), F32),
                   jax.ShapeDtypeStruct((bsz, n_heads, SSM_HEAD_DIM, SSM_STATE), F32)],
        compiler_params=_params(1),
        name="ssd_sample",
    )(x3, zg3, dt3, state_conv, state_ssm, conv_w, conv_b, alog_pad, dsk_x, ssm_norm_g, e32, eg)


def _mix_into(o_ref, lse_ref, idx, o_new, lse_new, first):
    if first:
        o_ref[idx] = o_new
        lse_ref[idx[1:]] = lse_new
        return
    o_old = o_ref[idx]
    l_old = lse_ref[idx[1:]]
    mx = jnp.maximum(l_old, lse_new)
    e0 = jnp.exp(l_old - mx)
    e1 = jnp.exp(lse_new - mx)
    s = e0 + e1
    o_ref[idx] = (e0 * o_old + e1 * o_new) / s
    lse_ref[idx[1:]] = mx + jnp.log(s)


def _attn_prompt_kernel(*refs, seqlen):
    tq = 128
    n_chunks = GROUP_WIDTH // LANES
    heads_per_chunk = LANES // ATTN_HEAD_DIM
    per_group = 3 * n_chunks
    n_in = per_group * len(DIL_PATTERNS)
    ins, o_refs, lse_refs = refs[:n_in], refs[n_in:n_in + n_chunks], refs[n_in + n_chunks:]
    scale = 1.0 / math.sqrt(ATTN_HEAD_DIM)
    lane = lax.broadcasted_iota(jnp.int32, (1, LANES), 1)
    hmask = [(lane // ATTN_HEAD_DIM) == e for e in range(heads_per_chunk)]

    for j, (_, dil) in enumerate(DIL_PATTERNS):
        grp = ins[per_group * j:per_group * (j + 1)]
        sub_len = seqlen // dil
        n_tiles = sub_len // tq
        win = min(2 * tq, sub_len)
        tile_bits = n_tiles.bit_length() - 1

        def unit(u, carry, j=j, dil=dil, grp=grp, n_tiles=n_tiles, win=win, tile_bits=tile_bits):
            r = lax.shift_right_logical(u, tile_bits)
            ti = lax.bitwise_and(u, n_tiles - 1)
            m0 = ti * tq
            ks = jnp.maximum(m0 - (win - tq), 0)
            if dil == 1:
                rows_q = pl.ds(pl.multiple_of(m0, tq), tq)
                rows_k = pl.ds(pl.multiple_of(ks, tq), win)
            else:
                rows_q = pl.ds(r + dil * m0, tq, stride=dil)
                rows_k = pl.ds(r + dil * ks, win, stride=dil)
            qpos = m0 + lax.broadcasted_iota(jnp.int32, (tq, win), 0)
            kpos = ks + lax.broadcasted_iota(jnp.int32, (tq, win), 1)
            valid = (kpos <= qpos) & (kpos >= qpos - 128)
            for c in range(n_chunks):
                q = grp[c][0, rows_q, :]
                k = grp[n_chunks + c][0, rows_k, :].astype(BF16)
                v = grp[2 * n_chunks + c][0, rows_k, :]
                o_acc = jnp.zeros((tq, LANES), F32)
                lse_acc = jnp.zeros((tq, LANES), F32)
                for e in range(heads_per_chunk):
                    qh = jnp.where(hmask[e], q, 0.0).astype(BF16)
                    s = _dot_nt(qh, k) * scale
                    s = jnp.where(valid, s, NEG_BIG)
                    m = jnp.max(s, axis=1, keepdims=True)
                    p = jnp.exp(s - m)
                    l = jnp.sum(p, axis=1, keepdims=True)
                    pn = (p / l).astype(BF16)
                    vh = jnp.where(hmask[e], v, 0.0).astype(BF16)
                    o_acc = o_acc + _dot(pn, vh)
                    lse_acc = jnp.where(hmask[e], m + jnp.log(l), lse_acc)
                _mix_into(o_refs[c], lse_refs[c], (0, rows_q, slice(None)), o_acc, lse_acc, j == 0)
            return carry

        lax.fori_loop(0, dil * n_tiles, unit, 0, unroll=2)


def _attn_prompt(qs, kvs, bsz, seqlen):
    gw = GROUP_WIDTH
    n_chunks = gw // LANES
    chunk = lambda c: pl.BlockSpec((1, seqlen, LANES), lambda b: (b, 0, c))
    args, specs = [], []
    for q, kv in zip(qs, kvs):
        q3 = q.reshape(bsz, seqlen, gw)
        kv3 = kv.reshape(bsz, seqlen, 2 * gw)
        args += [q3] * n_chunks + [kv3] * (2 * n_chunks)
        specs += [chunk(c) for c in range(n_chunks)] + [chunk(c) for c in range(2 * n_chunks)]
    out_spec = pl.BlockSpec((1, seqlen, LANES), lambda b: (b, 0, 0))
    shape = jax.ShapeDtypeStruct((bsz, seqlen, LANES), F32)
    outs = pl.pallas_call(
        functools.partial(_attn_prompt_kernel, seqlen=seqlen),
        grid=(bsz,),
        in_specs=specs,
        out_specs=[out_spec] * n_chunks,
        out_shape=[shape] * n_chunks,
        scratch_shapes=[pltpu.VMEM((seqlen, LANES), F32)] * n_chunks,
        compiler_params=_params(1),
        name="attn_prompt",
    )(*args)
    return [o.reshape(bsz * seqlen, LANES) for o in outs]


def _attn_sample_kernel(q0, q1, q2, kn0, kn1, kn2, c0, c1, c2, ef_ref, o_ref, *, seqlen):
    gw = GROUP_WIDTH
    scale = 1.0 / math.sqrt(ATTN_HEAD_DIM)
    ef = ef_ref[...]
    qs, kns, caches = (q0, q1, q2), (kn0, kn1, kn2), (c0, c1, c2)
    o_groups, l_groups = [], []
    for j, (_, dil) in enumerate(DIL_PATTERNS):
        q = qs[j][0]
        kvn = kns[j][0]
        kn = kvn[:, :gw]
        vn = kvn[:, gw:]
        cache = caches[j]
        n_rows = cache.shape[1]
        col = [0 if dil == 1 else 2 * gw * i for i in range(seqlen)]
        prods = [cache[0, :, col[i]:col[i] + gw] * q[i:i + 1, :] for i in range(seqlen)]
        s_all = _dot(jnp.concatenate(prods, axis=0).astype(BF16), ef) * scale
        new_pairs = [(i, t) for i in range(seqlen) for t in range(seqlen)
                     if (t <= i if dil == 1 else t == i)]
        n_new = -(-len(new_pairs) // 8) * 8
        pn = _stack_rows([q[i:i + 1, :] * kn[t:t + 1, :] for (i, t) in new_pairs], n_new)
        s_new = _dot(pn.astype(BF16), ef) * scale
        rowi = lax.broadcasted_iota(jnp.int32, (n_rows, gw), 0)
        o_rows, l_rows = [], []
        for i in range(seqlen):
            s = s_all[i * n_rows:(i + 1) * n_rows, :]
            if dil == 1:
                s = jnp.where(rowi >= i, s, NEG_BIG)
            m = jnp.max(s, axis=0, keepdims=True)
            mine = [idx for idx, (ii, _) in enumerate(new_pairs) if ii == i]
            for idx in mine:
                m = jnp.maximum(m, s_new[idx:idx + 1, :])
            p = jnp.exp(s - m)
            l = jnp.sum(p, axis=0, keepdims=True)
            vc = cache[0, :, col[i] + gw:col[i] + 2 * gw]
            acc = jnp.sum(p * vc, axis=0, keepdims=True)
            for idx in mine:
                t = new_pairs[idx][1]
                e = jnp.exp(s_new[idx:idx + 1, :] - m)
                l = l + e
                acc = acc + e * vn[t:t + 1, :]
            o_rows.append(acc / l)
            l_rows.append(m + jnp.log(l))
        o_groups.append(_stack_rows(o_rows, seqlen))
        l_groups.append(_stack_rows(l_rows, seqlen))
    mx = jnp.maximum(jnp.maximum(l_groups[0], l_groups[1]), l_groups[2])
    es = [jnp.exp(l - mx) for l in l_groups]
    o_ref[0] = (es[0] * o_groups[0] + es[1] * o_groups[1] + es[2] * o_groups[2]) / (es[0] + es[1] + es[2])


def _attn_sample(qs, kvs, caches, efull, layer, bsz, seqlen):
    gw = GROUP_WIDTH
    q3 = [q.reshape(bsz, seqlen, gw) for q in qs]
    kv3 = [kv.reshape(bsz, seqlen, 2 * gw) for kv in kvs]
    views, cache_specs = [], []
    for (window, dil), cache in zip(DIL_PATTERNS, caches):
        depth, b, rows = cache.shape[:3]
        used = min(dil, seqlen)
        view = cache.reshape(depth, b, rows // dil, dil, 2 * gw)[:, :, :, :used]
        view = view.reshape(depth, b, rows // dil, used * 2 * gw)
        views.append(view)
        cache_specs.append(pl.BlockSpec((None, 1, rows // dil, used * 2 * gw), lambda i: (layer, i, 0, 0)))
    q_spec = pl.BlockSpec((1, seqlen, gw), lambda i: (i, 0, 0))
    kv_spec = pl.BlockSpec((1, seqlen, 2 * gw), lambda i: (i, 0, 0))
    o = pl.pallas_call(
        functools.partial(_attn_sample_kernel, seqlen=seqlen),
        grid=(bsz,),
        in_specs=[q_spec] * 3 + [kv_spec] * 3 + cache_specs + [pl.BlockSpec(efull.shape, lambda i: (0, 0))],
        out_specs=q_spec,
        out_shape=jax.ShapeDtypeStruct((bsz, seqlen, gw), F32),
        compiler_params=_params(1),
        name="attn_sample",
    )(*q3, *kv3, *views, efull)
    return o.reshape(bsz * seqlen, gw)


def _merge_kernel(*refs, alpha, n_o):
    y_ref = refs[0]
    o_refs = refs[1:1 + n_o]
    g_ref, x_ref, wssm_ref, wattn_ref, wo_ref, lg_ref, lb_ref, xo_ref, xob_ref = refs[1 + n_o:]
    d = x_ref.shape[1]
    ys = _dot(y_ref[...].astype(BF16), wssm_ref[...])
    o = jnp.concatenate([r[...] for r in o_refs], axis=1) if n_o > 1 else o_refs[0][...]
    ya = _dot(o.astype(BF16), wattn_ref[...])
    g = g_ref[...].astype(F32)
    merged = g[:, :d] * ys + g[:, d:] * ya
    mix = _dot(merged.astype(BF16), wo_ref[...])
    out = _layer_norm(alpha * x_ref[...] + mix, lg_ref[...], lb_ref[...])
    xo_ref[...] = out
    xob_ref[...] = out.astype(BF16)


def _merge(y_ssm, o_parts, zg, x, w_ssm_br, w_attn_br, w_o, ln_g, ln_b, layer, tm, alpha):
    t, d = x.shape
    d_inner = w_ssm_br.shape[1]
    gw = GROUP_WIDTH
    row = lambda i: (i, 0)
    par = lambda i: (layer, 0, 0)
    return pl.pallas_call(
        functools.partial(_merge_kernel, alpha=alpha, n_o=len(o_parts)),
        grid=(t // tm,),
        in_specs=[pl.BlockSpec((tm, d_inner), row)] +
                 [pl.BlockSpec((tm, o.shape[1]), row) for o in o_parts] +
                 [pl.BlockSpec((tm, 2 * d), lambda i: (i, d_inner // (2 * d))),
                  pl.BlockSpec((tm, d), row),
                  pl.BlockSpec((None, d_inner, d), par),
                  pl.BlockSpec((None, gw, d), par),
                  pl.BlockSpec((None, d, d), par),
                  pl.BlockSpec((None, 1, d), par),
                  pl.BlockSpec((None, 1, d), par)],
        out_specs=[pl.BlockSpec((tm, d), row), pl.BlockSpec((tm, d), row)],
        out_shape=[jax.ShapeDtypeStruct((t, d), F32), jax.ShapeDtypeStruct((t, d), BF16)],
        compiler_params=_params(1),
        name="merge",
    )(y_ssm, *o_parts, zg, x, w_ssm_br, w_attn_br, w_o, ln_g, ln_b)


def _mlp_kernel(xb_ref, x_ref, wup_ref, wdn_ref, lg_ref, lb_ref, xo_ref, xob_ref, acc_ref, *, alpha):
    f = pl.program_id(1)
    nf = pl.num_programs(1)
    hid = _dot(xb_ref[...], wup_ref[...])
    hid = jnp.square(jnp.maximum(hid, 0.0))
    part = _dot(hid.astype(BF16), wdn_ref[...])

    @pl.when(f == 0)
    def _():
        acc_ref[...] = part

    @pl.when(f > 0)
    def _():
        acc_ref[...] += part

    @pl.when(f == nf - 1)
    def _():
        out = _layer_norm(alpha * x_ref[...] + acc_ref[...], lg_ref[...], lb_ref[...])
        xo_ref[...] = out
        xob_ref[...] = out.astype(BF16)


def _mlp(x_bf, x, w_up, w_down, ln_g, ln_b, layer, tm, alpha):
    t, d = x.shape
    d_ff = w_up.shape[2]
    tf = 1024
    return pl.pallas_call(
        functools.partial(_mlp_kernel, alpha=alpha),
        grid=(t // tm, d_ff // tf),
        in_specs=[pl.BlockSpec((tm, d), lambda i, f: (i, 0)),
                  pl.BlockSpec((tm, d), lambda i, f: (i, 0)),
                  pl.BlockSpec((None, d, tf), lambda i, f: (layer, 0, f)),
                  pl.BlockSpec((None, tf, d), lambda i, f: (layer, f, 0)),
                  pl.BlockSpec((None, 1, d), lambda i, f: (layer, 0, 0)),
                  pl.BlockSpec((None, 1, d), lambda i, f: (layer, 0, 0))],
        out_specs=[pl.BlockSpec((tm, d), lambda i, f: (i, 0)), pl.BlockSpec((tm, d), lambda i, f: (i, 0))],
        out_shape=[jax.ShapeDtypeStruct((t, d), F32), jax.ShapeDtypeStruct((t, d), BF16)],
        scratch_shapes=[pltpu.VMEM((tm, d), F32)],
        compiler_params=_params(2),
        name="mlp",
    )(x_bf, x, w_up, w_down, ln_g, ln_b)


def _rope_tables(pos):
    half = ROPE_DIM // 2
    inv_freq = jnp.power(ROPE_THETA, -jnp.arange(half, dtype=F32) / half)
    ang = pos.astype(F32)[:, None] * inv_freq[None, :]
    cos, sin = jnp.cos(ang), jnp.sin(ang)
    n = pos.shape[0]
    pad = jnp.zeros((n, ATTN_HEAD_DIM - ROPE_DIM), F32)
    zero = jnp.zeros((n, half), F32)
    c_head = jnp.concatenate([cos, cos, pad + 1.0], axis=1)
    s1_head = jnp.concatenate([zero, sin, pad], axis=1)
    s2_head = jnp.concatenate([-sin, zero, pad], axis=1)
    rep = LANES // ATTN_HEAD_DIM
    return tuple(jnp.tile(a, (1, rep)) for a in (c_head, s1_head, s2_head))


def kernel(x_prompt, x_sample, cache_kv_w128, cache_kv_w512, cache_kv_w2048, state_conv, state_ssm,
           w_in, conv_w, conv_b, dt_bias, a_log, d_skip, ssm_norm_g, w_ssm_br, w_attn_br, w_o,
           ln1_g, ln1_b, w_up, w_down, ln2_g, ln2_b):
    depth, d_model, _ = w_in.shape
    bsz, seqlen, _ = x_prompt.shape
    dec_b, dec_s, _ = x_sample.shape
    d_inner = w_ssm_br.shape[1]
    conv_dim = conv_w.shape[2]
    n_heads = dt_bias.shape[1]
    attn_w = 3 * GROUP_WIDTH
    gn = SSM_GROUPS * SSM_STATE
    alpha = (2 * depth) ** 0.25

    off_xbc = d_inner
    off_dt = off_xbc + conv_dim
    off_q = off_dt + n_heads
    off_k = off_q + attn_w
    off_v = off_k + attn_w
    off_g = off_v + attn_w

    w_xbc = w_in[:, :, off_xbc:off_dt].astype(BF16)
    w_zg = jnp.concatenate([w_in[:, :, :off_xbc], w_in[:, :, off_g:]], axis=-1).astype(BF16)
    cols = [w_in[:, :, off_q + j * GROUP_WIDTH:off_q + (j + 1) * GROUP_WIDTH] for j in range(3)]
    for j in range(3):
        cols.append(w_in[:, :, off_k + j * GROUP_WIDTH:off_k + (j + 1) * GROUP_WIDTH])
        cols.append(w_in[:, :, off_v + j * GROUP_WIDTH:off_v + (j + 1) * GROUP_WIDTH])
    cols.append(jnp.pad(w_in[:, :, off_dt:off_q], ((0, 0), (0, 0), (0, LANES - n_heads))))
    w_b = jnp.concatenate(cols, axis=-1).astype(BF16)
    w_ssm_bf = w_ssm_br.astype(BF16)
    w_attn_bf = w_attn_br.astype(BF16)
    w_o_bf = w_o.astype(BF16)
    w_up_bf = w_up.astype(BF16)
    w_down_bf = w_down.astype(BF16)

    r3 = lambda a: a.reshape(depth, 1, a.shape[-1])
    dtb = r3(jnp.pad(dt_bias, ((0, 0), (0, LANES - n_heads))))
    alog_pad = r3(jnp.pad(a_log, ((0, 0), (0, LANES - n_heads))))
    dsk_x = r3(jnp.repeat(d_skip, SSM_HEAD_DIM, axis=1))
    conv_b3, ng3 = r3(conv_b), r3(ssm_norm_g)
    ln1_g3, ln1_b3, ln2_g3, ln2_b3 = r3(ln1_g), r3(ln1_b), r3(ln2_g), r3(ln2_b)

    e32 = (jnp.arange(LANES)[:, None] == jnp.arange(d_inner)[None, :] // SSM_HEAD_DIM).astype(BF16)
    eg = (jnp.arange(gn)[:, None] // SSM_STATE ==
          jnp.arange(d_inner)[None, :] // (d_inner // SSM_GROUPS)).astype(BF16)
    efull = (jnp.arange(GROUP_WIDTH)[:, None] // ATTN_HEAD_DIM ==
             jnp.arange(GROUP_WIDTH)[None, :] // ATTN_HEAD_DIM).astype(BF16)

    rope_p = _rope_tables(jnp.arange(seqlen))
    rope_s = _rope_tables(PAST_LEN + jnp.arange(dec_b * dec_s) % dec_s)

    tp = bsz * seqlen
    ts = dec_b * dec_s
    xp = x_prompt.reshape(tp, d_model)
    xs = x_sample.reshape(ts, d_model)
    xp_bf, xs_bf = xp.astype(BF16), xs.astype(BF16)
    caches = (cache_kv_w128, cache_kv_w512, cache_kv_w2048)

    p_kv = ([], [], [])
    s_kv = ([], [], [])
    p_conv, p_ssm, s_conv, s_ssm = [], [], [], []
    tm_p, tm_s, tn = 1024, ts, 2048
    for l in range(depth):
        xbc_act, tail = _proj_xbc_conv(xp_bf, w_xbc, conv_w, conv_b3, l, tm_p, tn, bsz, seqlen)
        zg = _proj_a(xp_bf, w_zg, l, tm_p, tn, _proj_zg_kernel, BF16, "proj_zg")
        q0, q1, q2, kv0, kv1, kv2, dt = _proj_b(xp_bf, w_b, dtb, rope_p, l, 512)
        y_ssm, h_new = _ssd_prompt(xbc_act, zg, dt, alog_pad, dsk_x, ng3, l, bsz, seqlen)
        o_attn = _attn_prompt((q0, q1, q2), (kv0, kv1, kv2), bsz, seqlen)
        x1, x1_bf = _merge(y_ssm, o_attn, zg, xp, w_ssm_bf, w_attn_bf, w_o_bf, ln1_g3, ln1_b3, l, 512, alpha)
        xp, xp_bf = _mlp(x1_bf, x1, w_up_bf, w_down_bf, ln2_g3, ln2_b3, l, tm_p, alpha)
        for j, ((window, _), kvj) in enumerate(zip(DIL_PATTERNS, (kv0, kv1, kv2))):
            rows = kvj.reshape(bsz, seqlen, 2, HEADS_PER_GROUP, ATTN_HEAD_DIM)
            p_kv[j].append(rows[:, seqlen - min(window, seqlen):])
        p_conv.append(tail[:, SUBLANES - (CONV_WIDTH - 1):, :])
        p_ssm.append(h_new)

        xbc_raw = _proj_a(xs_bf, w_xbc, l, tm_s, tn, _proj_plain_kernel, F32, "proj_xbc")
        zg = _proj_a(xs_bf, w_zg, l, tm_s, tn, _proj_zg_kernel, BF16, "proj_zg")
        q0, q1, q2, kv0, kv1, kv2, dt = _proj_b(xs_bf, w_b, dtb, rope_s, l, tm_s)
        y_ssm, h_new = _ssd_sample(xbc_raw, zg, dt, state_conv, state_ssm, conv_w, conv_b3, alog_pad, dsk_x, ng3,
                                   e32, eg, l, dec_b, dec_s)
        o_attn = _attn_sample((q0, q1, q2), (kv0, kv1, kv2), caches, efull, l, dec_b, dec_s)
        x1, x1_bf = _merge(y_ssm.reshape(ts, d_inner), [o_attn], zg, xs, w_ssm_bf, w_attn_bf, w_o_bf,
                           ln1_g3, ln1_b3, l, tm_s, alpha)
        xs, xs_bf = _mlp(x1_bf, x1, w_up_bf, w_down_bf, ln2_g3, ln2_b3, l, tm_s, alpha)
        for j, kvj in enumerate((kv0, kv1, kv2)):
            s_kv[j].append(kvj.reshape(dec_b, dec_s, 2, HEADS_PER_GROUP, ATTN_HEAD_DIM))
        s_conv.append(xbc_raw.reshape(dec_b, dec_s, conv_dim)[:, dec_s - (CONV_WIDTH - 1):])
        s_ssm.append(h_new)

    return (xp.reshape(bsz, seqlen, d_model), xs.reshape(dec_b, dec_s, d_model),
            jnp.stack(p_kv[0]), jnp.stack(p_kv[1]), jnp.stack(p_kv[2]), jnp.stack(p_conv), jnp.stack(p_ssm),
            jnp.stack(s_kv[0]), jnp.stack(s_kv[1]), jnp.stack(s_kv[2]), jnp.stack(s_conv), jnp.stack(s_ssm))
```

```python
import functools
import math

import jax
import jax.numpy as jnp
from jax import lax
from jax.experimental import pallas as pl
from jax.experimental.pallas import tpu as pltpu

F32 = jnp.float32
BF16 = jnp.bfloat16

SSM_HEAD_DIM = 64
SSM_GROUPS = 8
SSM_STATE = 128
CONV_WIDTH = 4
SSD_CHUNK = 128
CONV_SUB = 256
ATTN_SAMPLE_BATCH_TILE = 4
ATTN_HEAD_DIM = 64
HEADS_PER_GROUP = 4
GROUP_WIDTH = HEADS_PER_GROUP * ATTN_HEAD_DIM
DIL_PATTERNS = ((128, 1), (512, 4), (2048, 16))
ROPE_DIM = 16
ROPE_THETA = 500000.0
LN_EPS = 1e-5
RMS_EPS = 1e-5
PAST_LEN = 2048
NEG_BIG = -1e30

LANES = 128
SUBLANES = 8
V7X_VMEM_LIMIT = 56 * 1024 * 1024


def _params(n_axes):
    return pltpu.CompilerParams(dimension_semantics=("arbitrary",) * n_axes,
                                vmem_limit_bytes=V7X_VMEM_LIMIT)


def _dot(a, b):
    return jnp.dot(a, b, preferred_element_type=F32)


def _dot_nt(a, b):
    return lax.dot_general(a, b, (((1,), (1,)), ((), ())), preferred_element_type=F32)


def _dot_tn(a, b):
    return lax.dot_general(a, b, (((0,), (0,)), ((), ())), preferred_element_type=F32)


def _split3(v):
    hi = v.astype(BF16)
    r1 = v - hi.astype(F32)
    mid = r1.astype(BF16)
    lo = (r1 - mid.astype(F32)).astype(BF16)
    return hi, mid, lo


def _expand_exact(v, e):
    hi, mid, lo = _split3(v)
    return _dot(hi, e) + _dot(mid, e) + _dot(lo, e)


def _stack_rows(rows, n):
    w = rows[0].shape[1]
    ri = lax.broadcasted_iota(jnp.int32, (n, w), 0)
    out = jnp.zeros((n, w), F32)
    for k, row in enumerate(rows):
        out = jnp.where(ri == k, row, out)
    return out


def _silu(v):
    return v * jax.nn.sigmoid(v)


def _layer_norm(r, g, b):
    mu = jnp.mean(r, axis=-1, keepdims=True)
    cen = r - mu
    var = jnp.mean(cen * cen, axis=-1, keepdims=True)
    return cen * lax.rsqrt(var + LN_EPS) * g + b


def _gated_group_norm(y, zs, ng, out_ref, lead):
    hgt = y * zs
    gw = hgt.shape[1] // SSM_GROUPS
    for g in range(SSM_GROUPS):
        seg = hgt[:, g * gw:(g + 1) * gw]
        ms = jnp.mean(seg * seg, axis=-1, keepdims=True)
        val = seg * lax.rsqrt(ms + RMS_EPS) * ng[:, g * gw:(g + 1) * gw]
        out_ref[lead + (slice(None), slice(g * gw, (g + 1) * gw))] = val.astype(out_ref.dtype)


def _proj_plain_kernel(x_ref, w_ref, o_ref):
    o_ref[...] = _dot(x_ref[...], w_ref[...])


def _proj_zg_kernel(x_ref, w_ref, o_ref):
    j = pl.program_id(0)
    w = w_ref[...]
    tm = o_ref.shape[0]
    sub = min(CONV_SUB, tm)
    for s in range(tm // sub):
        acc = _dot(x_ref[s * sub:(s + 1) * sub, :], w)
        sig = jax.nn.sigmoid(acc)
        o_ref[s * sub:(s + 1) * sub, :] = jnp.where(j == 0, acc * sig, sig).astype(o_ref.dtype)


def _proj_a(x_bf, w, layer, tm, tn, body, out_dtype, name):
    t, d = x_bf.shape
    n = w.shape[2]
    return pl.pallas_call(
        body,
        grid=(n // tn, t // tm),
        in_specs=[pl.BlockSpec((tm, d), lambda j, i: (i, 0)),
                  pl.BlockSpec((None, d, tn), lambda j, i: (layer, 0, j))],
        out_specs=pl.BlockSpec((tm, tn), lambda j, i: (i, j)),
        out_shape=jax.ShapeDtypeStruct((t, n), out_dtype),
        compiler_params=_params(2),
        name=name,
    )(x_bf, w)


def _proj_xbc_conv_kernel(x_ref, w_ref, cw_ref, cb_ref, o_ref, tail_ref, carry_ref, *, tiles_per_seq):
    i = pl.program_id(1)
    tm, tn = o_ref.shape
    pos_in_seq = i % tiles_per_seq
    prev = jnp.where(pos_in_seq == 0, 0.0, carry_ref[...])
    cw = cw_ref[...]
    cb = cb_ref[...]
    w = w_ref[...]
    row8 = lax.broadcasted_iota(jnp.int32, (SUBLANES, tn), 0)
    for s in range(tm // CONV_SUB):
        raw = _dot(x_ref[s * CONV_SUB:(s + 1) * CONV_SUB, :], w)
        conv = cb + raw * cw[CONV_WIDTH - 1:CONV_WIDTH, :]
        for k in range(1, CONV_WIDTH):
            sh = pltpu.roll(raw, k, 0)
            top = jnp.where(row8 < k, pltpu.roll(prev, k, 0), sh[0:SUBLANES, :])
            sh = jnp.concatenate([top, sh[SUBLANES:, :]], axis=0)
            conv = conv + sh * cw[CONV_WIDTH - 1 - k:CONV_WIDTH - k, :]
        o_ref[s * CONV_SUB:(s + 1) * CONV_SUB, :] = _silu(conv).astype(o_ref.dtype)
        prev = raw[CONV_SUB - SUBLANES:CONV_SUB, :]
    carry_ref[...] = prev

    @pl.when(pos_in_seq == tiles_per_seq - 1)
    def _():
        tail_ref[0] = prev


def _proj_xbc_conv(x_bf, w, conv_w, conv_b, layer, tm, tn, bsz, seqlen):
    t, d = x_bf.shape
    n = w.shape[2]
    tiles_per_seq = seqlen // tm
    return pl.pallas_call(
        functools.partial(_proj_xbc_conv_kernel, tiles_per_seq=tiles_per_seq),
        grid=(n // tn, t // tm),
        in_specs=[pl.BlockSpec((tm, d), lambda j, i: (i, 0)),
                  pl.BlockSpec((None, d, tn), lambda j, i: (layer, 0, j)),
                  pl.BlockSpec((None, CONV_WIDTH, tn), lambda j, i: (layer, 0, j)),
                  pl.BlockSpec((None, 1, tn), lambda j, i: (layer, 0, j))],
        out_specs=[pl.BlockSpec((tm, tn), lambda j, i: (i, j)),
                   pl.BlockSpec((1, SUBLANES, tn), lambda j, i: (i // tiles_per_seq, 0, j))],
        out_shape=[jax.ShapeDtypeStruct((t, n), BF16),
                   jax.ShapeDtypeStruct((bsz, SUBLANES, n), F32)],
        scratch_shapes=[pltpu.VMEM((SUBLANES, tn), F32)],
        compiler_params=_params(2),
        name="proj_xbc_conv",
    )(x_bf, w, conv_w, conv_b)


def _proj_b_kernel(x_ref, w_ref, c_ref, s1_ref, s2_ref, dtb_ref,
                   q0, q1, q2, kv0, kv1, kv2, dt_ref):
    acc = _dot(x_ref[...], w_ref[...])
    c = c_ref[...]
    s1 = s1_ref[...]
    s2 = s2_ref[...]

    def rot(v):
        return v * c + pltpu.roll(v, 8, 1) * s1 + pltpu.roll(v, LANES - 8, 1) * s2

    qs = (q0, q1, q2)
    kvs = (kv0, kv1, kv2)
    q_total = 3 * GROUP_WIDTH
    for j in range(3):
        for u in range(2):
            lo = j * GROUP_WIDTH + u * LANES
            qs[j][:, u * LANES:(u + 1) * LANES] = rot(acc[:, lo:lo + LANES])
        base = q_total + j * 2 * GROUP_WIDTH
        for u in range(2):
            lo = base + u * LANES
            kvs[j][:, u * LANES:(u + 1) * LANES] = rot(acc[:, lo:lo + LANES])
        kvs[j][:, GROUP_WIDTH:2 * GROUP_WIDTH] = acc[:, base + GROUP_WIDTH:base + 2 * GROUP_WIDTH]
    dt_lo = q_total + 3 * 2 * GROUP_WIDTH
    v = acc[:, dt_lo:dt_lo + LANES] + dtb_ref[...]
    dt_ref[...] = jnp.maximum(v, 0.0) + jnp.log1p(jnp.exp(-jnp.abs(v)))


def _proj_b(x_bf, w_b, dtb, rope, layer, tm):
    t, d = x_bf.shape
    n = w_b.shape[2]
    cos_t, sin1_t, sin2_t = rope
    npos = cos_t.shape[0] // tm
    tab = pl.BlockSpec((tm, LANES), lambda i: (i % npos, 0))
    q_shape = jax.ShapeDtypeStruct((t, GROUP_WIDTH), F32)
    kv_shape = jax.ShapeDtypeStruct((t, 2 * GROUP_WIDTH), F32)
    q_spec = pl.BlockSpec((tm, GROUP_WIDTH), lambda i: (i, 0))
    kv_spec = pl.BlockSpec((tm, 2 * GROUP_WIDTH), lambda i: (i, 0))
    return pl.pallas_call(
        _proj_b_kernel,
        grid=(t // tm,),
        in_specs=[pl.BlockSpec((tm, d), lambda i: (i, 0)),
                  pl.BlockSpec((None, d, n), lambda i: (layer, 0, 0)),
                  tab, tab, tab,
                  pl.BlockSpec((None, 1, LANES), lambda i: (layer, 0, 0))],
        out_specs=[q_spec, q_spec, q_spec, kv_spec, kv_spec, kv_spec,
                   pl.BlockSpec((tm, LANES), lambda i: (i, 0))],
        out_shape=[q_shape, q_shape, q_shape, kv_shape, kv_shape, kv_shape,
                   jax.ShapeDtypeStruct((t, LANES), F32)],
        compiler_params=_params(1),
        name="proj_b",
    )(x_bf, w_b, cos_t, sin1_t, sin2_t, dtb)


def _ssd_prompt_kernel(xbc_ref, zs_ref, dt_ref, alog_ref, dsk_ref, ng_ref,
                       y_ref, h_ref, ht_ref, ybuf_ref, *, n_heads):
    q = SSD_CHUNK
    d_inner = n_heads * SSM_HEAD_DIM
    gn = SSM_GROUPS * SSM_STATE
    gw = d_inner // SSM_GROUPS
    heads_per_group = n_heads // SSM_GROUPS
    c = pl.program_id(1)
    nc = pl.num_programs(1)

    @pl.when(c == 0)
    def _():
        ht_ref[...] = jnp.zeros(ht_ref.shape, F32)

    lane = lax.broadcasted_iota(jnp.int32, (q, LANES), 1)
    dt = jnp.where(lane < n_heads, dt_ref[...], 0.0)
    a = -jnp.exp(alog_ref[...])
    da = dt * a
    ri = lax.broadcasted_iota(jnp.int32, (q, q), 0)
    ci = lax.broadcasted_iota(jnp.int32, (q, q), 1)
    tril = ri >= ci
    acum = lax.dot_general(tril.astype(F32), da, (((1,), (0,)), ((), ())),
                           precision=lax.Precision.HIGHEST, preferred_element_type=F32)
    acum_t = acum.T
    dt_t = dt.T
    coef_t = dt_t * jnp.exp(acum_t[:, q - 1:q] - acum_t)
    cd = jnp.exp(acum[q - 1:q, :])

    glane = lax.broadcasted_iota(jnp.int32, (1, gw), 1)
    hmask = [(glane // SSM_HEAD_DIM) == r for r in range(heads_per_group)]
    zero_bf = jnp.zeros((), BF16)

    for g in range(SSM_GROUPS):
        cg_bf = xbc_ref[:, d_inner + gn + g * SSM_STATE:d_inner + gn + (g + 1) * SSM_STATE]
        bg_bf = xbc_ref[:, d_inner + g * SSM_STATE:d_inner + (g + 1) * SSM_STATE]
        cg = cg_bf.astype(F32)
        bg_t = bg_bf.astype(F32).T
        cb = _dot_nt(cg_bf, bg_bf)
        xg = xbc_ref[:, g * gw:(g + 1) * gw]
        htg = ht_ref[g]
        rhs_full = jnp.concatenate([xg, htg.astype(BF16)], axis=0)
        y_g = jnp.zeros((q, gw), F32)
        st_g = jnp.zeros((SSM_STATE, gw), F32)
        cd_g = jnp.zeros((1, gw), F32)
        for r in range(heads_per_group):
            h = g * heads_per_group + r
            acol = jnp.broadcast_to(acum[:, h:h + 1], (q, q))
            seg = acol - acum_t[h:h + 1, :]
            dec = jnp.exp(jnp.where(tril, seg, -jnp.inf))
            m = cb * dec * dt_t[h:h + 1, :]
            eac = jnp.exp(acol) * cg
            lhs = jnp.concatenate([m, eac], axis=1).astype(BF16)
            rhs = jnp.where(hmask[r], rhs_full, zero_bf)
            y_g = y_g + _dot(lhs, rhs)
            bs = (bg_t * coef_t[h:h + 1, :]).astype(BF16)
            st_g = st_g + _dot(bs, jnp.where(hmask[r], xg, zero_bf))
            cd_g = jnp.where(hmask[r], jnp.broadcast_to(cd[:, h:h + 1], (1, gw)), cd_g)
        ht_ref[g] = htg * cd_g + st_g
        ybuf_ref[:, g * gw:(g + 1) * gw] = y_g

    y = ybuf_ref[...] + dsk_ref[...] * xbc_ref[:, :d_inner].astype(F32)
    _gated_group_norm(y, zs_ref[...].astype(F32), ng_ref[...], y_ref, ())

    @pl.when(c == nc - 1)
    def _():
        pairs_per_group = heads_per_group // 2
        for g in range(SSM_GROUPS):
            for pr in range(pairs_per_group):
                blk = ht_ref[g, :, pr * LANES:(pr + 1) * LANES].T
                h0 = g * heads_per_group + 2 * pr
                h_ref[0, h0:h0 + 2] = blk.reshape(2, SSM_HEAD_DIM, SSM_STATE)


def _ssd_prompt(xbc_act, zg, dt, alog_pad, dsk_x, ssm_norm_g, layer, bsz, seqlen):
    t, conv_dim = xbc_act.shape
    d_inner = ssm_norm_g.shape[2]
    n_heads = d_inner // SSM_HEAD_DIM
    q = SSD_CHUNK
    nc = seqlen // q
    row = lambda b, c: (b * nc + c, 0)
    par = lambda b, c: (layer, 0, 0)
    return pl.pallas_call(
        functools.partial(_ssd_prompt_kernel, n_heads=n_heads),
        grid=(bsz, nc),
        in_specs=[pl.BlockSpec((q, conv_dim), row),
                  pl.BlockSpec((q, d_inner), row),
                  pl.BlockSpec((q, LANES), row),
                  pl.BlockSpec((None, 1, LANES), par),
                  pl.BlockSpec((None, 1, d_inner), par),
                  pl.BlockSpec((None, 1, d_inner), par)],
        out_specs=[pl.BlockSpec((q, d_inner), row),
                   pl.BlockSpec((1, n_heads, SSM_HEAD_DIM, SSM_STATE), lambda b, c: (b, 0, 0, 0))],
        out_shape=[jax.ShapeDtypeStruct((t, d_inner), BF16),
                   jax.ShapeDtypeStruct((bsz, n_heads, SSM_HEAD_DIM, SSM_STATE), F32)],
        scratch_shapes=[pltpu.VMEM((SSM_GROUPS, SSM_STATE, d_inner // SSM_GROUPS), F32),
                        pltpu.VMEM((q, d_inner), F32)],
        compiler_params=_params(2),
        name="ssd_prompt",
    )(xbc_act, zg, dt, alog_pad, dsk_x, ssm_norm_g)


def _ssd_sample_kernel(xbc_ref, zs_ref, dt_ref, sc_ref, h0_ref, cw_ref, cb_ref, alog_ref, dsk_ref, ng_ref,
                       e32_ref, eg_ref, y_ref, h_ref, *, n_heads, seqlen):
    d_inner = n_heads * SSM_HEAD_DIM
    gn = SSM_GROUPS * SSM_STATE
    sc = sc_ref[0]
    x = xbc_ref[0]
    xp = [sc[s:s + 1, :] for s in range(CONV_WIDTH - 1)] + [x[s:s + 1, :] for s in range(seqlen)]
    cw = cw_ref[...]
    cb = cb_ref[...]
    xs, bm, cm = [], [], []
    for t in range(seqlen):
        acc = cb
        for tap in range(CONV_WIDTH):
            acc = acc + xp[t + tap] * cw[tap:tap + 1, :]
        act = _silu(acc)
        xs.append(act[:, :d_inner])
        bm.append(act[:, d_inner:d_inner + gn])
        cm.append(act[:, d_inner + gn:])

    lane = lax.broadcasted_iota(jnp.int32, (1, LANES), 1)
    dtv = dt_ref[0]
    dt = [jnp.where(lane < n_heads, dtv[t:t + 1, :], 0.0) for t in range(seqlen)]
    a = -jnp.exp(alog_ref[...])
    acum = []
    run = jnp.zeros((1, LANES), F32)
    for t in range(seqlen):
        run = run + dt[t] * a
        acum.append(run)
    last = seqlen - 1

    pairs = [(t, j) for t in range(seqlen) for j in range(t + 1)]
    rows = [jnp.exp(acum[t] - acum[j]) * dt[j] for (t, j) in pairs]
    ea_row0 = len(rows)
    rows += [jnp.exp(acum[t]) for t in range(seqlen)]
    ce_row0 = len(rows)
    rows += [jnp.exp(acum[last] - acum[j]) * dt[j] for j in range(seqlen)]
    n_rows = -(-len(rows) // 8) * 8
    rx = _expand_exact(_stack_rows(rows, n_rows), e32_ref[...])

    n_prod = -(-len(pairs) // 8) * 8
    prods = _stack_rows([cm[t] * bm[j] for (t, j) in pairs], n_prod)
    p_hi = prods.astype(BF16)
    p_lo = (prods - p_hi.astype(F32)).astype(BF16)
    eg = eg_ref[...]
    cbx = _dot(p_hi, eg) + _dot(p_lo, eg)

    crow = _stack_rows(cm, 8).astype(BF16)
    n_pairs = n_heads // 2
    heads_per_group = n_heads // SSM_GROUPS
    yoff_parts = []
    for k in range(n_pairs):
        g = (2 * k) // heads_per_group
        hp = h0_ref[0, 2 * k:2 * k + 2].reshape(2 * SSM_HEAD_DIM, SSM_STATE)
        yoff_parts.append(_dot_nt(crow[:, g * SSM_STATE:(g + 1) * SSM_STATE], hp.astype(BF16)))
    yoff = jnp.concatenate(yoff_parts, axis=1)

    dsk = dsk_ref[...]
    ys = []
    for t in range(seqlen):
        yt = rx[ea_row0 + t:ea_row0 + t + 1, :] * yoff[t:t + 1, :] + dsk * xs[t]
        for idx, (tt, j) in enumerate(pairs):
            if tt == t:
                yt = yt + cbx[idx:idx + 1, :] * rx[idx:idx + 1, :] * xs[j]
        ys.append(yt)
    y = _stack_rows(ys, seqlen)
    _gated_group_norm(y, zs_ref[0].astype(F32), ng_ref[...], y_ref, (0,))

    cdx = rx[ea_row0 + last:ea_row0 + last + 1, :]
    cd_hi = cdx.astype(BF16).astype(F32)
    cd_r = cdx - cd_hi
    cd_mid = cd_r.astype(BF16).astype(F32)
    cd_lo = cd_r - cd_mid
    lhs_rows = [rx[ce_row0 + j:ce_row0 + j + 1, :] * xs[j] for j in range(seqlen)] + [cd_hi, cd_mid, cd_lo]
    lhs = _stack_rows(lhs_rows, 8).astype(BF16)
    zero_n = jnp.zeros((1, SSM_STATE), F32)
    one_n = jnp.ones((1, SSM_STATE), F32)
    rhs_by_group = []
    for g in range(SSM_GROUPS):
        rws = [jnp.concatenate([bm[j][:, g * SSM_STATE:(g + 1) * SSM_STATE], zero_n], axis=1)
               for j in range(seqlen)]
        rws += [jnp.concatenate([zero_n, one_n], axis=1)] * 3
        rhs_by_group.append(_stack_rows(rws, 8).astype(BF16))
    for k in range(n_pairs):
        g = (2 * k) // heads_per_group
        out = _dot_tn(lhs[:, k * LANES:(k + 1) * LANES], rhs_by_group[g])
        hp = h0_ref[0, 2 * k:2 * k + 2].reshape(2 * SSM_HEAD_DIM, SSM_STATE)
        hn = out[:, SSM_STATE:] * hp + out[:, :SSM_STATE]
        h_ref[0, 2 * k:2 * k + 2] = hn.reshape(2, SSM_HEAD_DIM, SSM_STATE)


def _ssd_sample(xbc_raw, zg, dt, state_conv, state_ssm, conv_w, conv_b, alog_pad, dsk_x, ssm_norm_g,
                e32, eg, layer, bsz, seqlen):
    conv_dim = conv_w.shape[2]
    d_inner = ssm_norm_g.shape[2]
    n_heads = d_inner // SSM_HEAD_DIM
    x3 = xbc_raw.reshape(bsz, seqlen, conv_dim)
    zg3 = zg.reshape(bsz, seqlen, zg.shape[1])
    dt3 = dt.reshape(bsz, seqlen, LANES)
    par = lambda b: (layer, 0, 0)
    return pl.pallas_call(
        functools.partial(_ssd_sample_kernel, n_heads=n_heads, seqlen=seqlen),
        grid=(bsz,),
        in_specs=[pl.BlockSpec((1, seqlen, conv_dim), lambda b: (b, 0, 0)),
                  pl.BlockSpec((1, seqlen, d_inner), lambda b: (b, 0, 0)),
                  pl.BlockSpec((1, seqlen, LANES), lambda b: (b, 0, 0)),
                  pl.BlockSpec((None, 1, CONV_WIDTH - 1, conv_dim), lambda b: (layer, b, 0, 0)),
                  pl.BlockSpec((None, 1, n_heads, SSM_HEAD_DIM, SSM_STATE), lambda b: (layer, b, 0, 0, 0)),
                  pl.BlockSpec((None, CONV_WIDTH, conv_dim), par),
                  pl.BlockSpec((None, 1, conv_dim), par),
                  pl.BlockSpec((None, 1, LANES), par),
                  pl.BlockSpec((None, 1, d_inner), par),
                  pl.BlockSpec((None, 1, d_inner), par),
                  pl.BlockSpec(e32.shape, lambda b: (0, 0)),
                  pl.BlockSpec(eg.shape, lambda b: (0, 0))],
        out_specs=[pl.BlockSpec((1, seqlen, d_inner), lambda b: (b, 0, 0)),
                   pl.BlockSpec((1, n_heads, SSM_HEAD_DIM, SSM_STATE), lambda b: (b, 0, 0, 0))],
        out_shape=[jax.ShapeDtypeStruct((bsz, seqlen, d_inner), F32),
                   jax.ShapeDtypeStruct((bsz, n_heads, SSM_HEAD_DIM, SSM_STATE), F32)],
        compiler_params=_params(1),
        name="ssd_sample",
    )(x3, zg3, dt3, state_conv, state_ssm, conv_w, conv_b, alog_pad, dsk_x, ssm_norm_g, e32, eg)


def _mix_into(o_ref, lse_ref, idx, o_new, lse_new, first):
    if first:
        o_ref[idx] = o_new
        lse_ref[idx[1:]] = lse_new
        return
    o_old = o_ref[idx]
    l_old = lse_ref[idx[1:]]
    mx = jnp.maximum(l_old, lse_new)
    e0 = jnp.exp(l_old - mx)
    e1 = jnp.exp(lse_new - mx)
    s = e0 + e1
    o_ref[idx] = (e0 * o_old + e1 * o_new) / s
    lse_ref[idx[1:]] = mx + jnp.log(s)


def _attn_prompt_kernel(*refs, seqlen):
    tq = 128
    n_chunks = GROUP_WIDTH // LANES
    heads_per_chunk = LANES // ATTN_HEAD_DIM
    per_group = 3 * n_chunks
    n_in = per_group * len(DIL_PATTERNS)
    ins, o_refs, lse_refs = refs[:n_in], refs[n_in:n_in + n_chunks], refs[n_in + n_chunks:]
    scale = 1.0 / math.sqrt(ATTN_HEAD_DIM)
    lane = lax.broadcasted_iota(jnp.int32, (1, LANES), 1)
    hmask = [(lane // ATTN_HEAD_DIM) == e for e in range(heads_per_chunk)]

    for j, (_, dil) in enumerate(DIL_PATTERNS):
        grp = ins[per_group * j:per_group * (j + 1)]
        sub_len = seqlen // dil
        n_tiles = sub_len // tq
        win = min(2 * tq, sub_len)
        tile_bits = n_tiles.bit_length() - 1

        def unit(u, carry, j=j, dil=dil, grp=grp, n_tiles=n_tiles, win=win, tile_bits=tile_bits):
            r = lax.shift_right_logical(u, tile_bits)
            ti = lax.bitwise_and(u, n_tiles - 1)
            m0 = ti * tq
            ks = jnp.maximum(m0 - (win - tq), 0)
            if dil == 1:
                rows_q = pl.ds(pl.multiple_of(m0, tq), tq)
                rows_k = pl.ds(pl.multiple_of(ks, tq), win)
            else:
                rows_q = pl.ds(r + dil * m0, tq, stride=dil)
                rows_k = pl.ds(r + dil * ks, win, stride=dil)
            qpos = m0 + lax.broadcasted_iota(jnp.int32, (tq, win), 0)
            kpos = ks + lax.broadcasted_iota(jnp.int32, (tq, win), 1)
            valid = (kpos <= qpos) & (kpos >= qpos - 128)
            for c in range(n_chunks):
                q = grp[c][0, rows_q, :]
                k = grp[n_chunks + c][0, rows_k, :].astype(BF16)
                v = grp[2 * n_chunks + c][0, rows_k, :]
                o_acc = jnp.zeros((tq, LANES), F32)
                lse_acc = jnp.zeros((tq, LANES), F32)
                for e in range(heads_per_chunk):
                    qh = jnp.where(hmask[e], q, 0.0).astype(BF16)
                    s = _dot_nt(qh, k) * scale
                    s = jnp.where(valid, s, NEG_BIG)
                    m = jnp.max(s, axis=1, keepdims=True)
                    p = jnp.exp(s - m)
                    l = jnp.sum(p, axis=1, keepdims=True)
                    pn = (p / l).astype(BF16)
                    vh = jnp.where(hmask[e], v, 0.0).astype(BF16)
                    o_acc = o_acc + _dot(pn, vh)
                    lse_acc = jnp.where(hmask[e], m + jnp.log(l), lse_acc)
                _mix_into(o_refs[c], lse_refs[c], (0, rows_q, slice(None)), o_acc, lse_acc, j == 0)
            return carry

        lax.fori_loop(0, dil * n_tiles, unit, 0, unroll=2)


def _attn_prompt(qs, kvs, bsz, seqlen):
    gw = GROUP_WIDTH
    n_chunks = gw // LANES
    chunk = lambda c: pl.BlockSpec((1, seqlen, LANES), lambda b: (b, 0, c))
    args, specs = [], []
    for q, kv in zip(qs, kvs):
        q3 = q.reshape(bsz, seqlen, gw)
        kv3 = kv.reshape(bsz, seqlen, 2 * gw)
        args += [q3] * n_chunks + [kv3] * (2 * n_chunks)
        specs += [chunk(c) for c in range(n_chunks)] + [chunk(c) for c in range(2 * n_chunks)]
    out_spec = pl.BlockSpec((1, seqlen, LANES), lambda b: (b, 0, 0))
    shape = jax.ShapeDtypeStruct((bsz, seqlen, LANES), F32)
    outs = pl.pallas_call(
        functools.partial(_attn_prompt_kernel, seqlen=seqlen),
        grid=(bsz,),
        in_specs=specs,
        out_specs=[out_spec] * n_chunks,
        out_shape=[shape] * n_chunks,
        scratch_shapes=[pltpu.VMEM((seqlen, LANES), F32)] * n_chunks,
        compiler_params=_params(1),
        name="attn_prompt",
    )(*args)
    return [o.reshape(bsz * seqlen, LANES) for o in outs]


def _attn_sample_kernel(q0, q1, q2, kn0, kn1, kn2, c0, c1, c2, o_ref, *, seqlen):
    scale = 1.0 / math.sqrt(ATTN_HEAD_DIM)
    gw = GROUP_WIDTH
    qs, kns, caches = (q0, q1, q2), (kn0, kn1, kn2), (c0, c1, c2)
    bt, rows = q0.shape[0], q0.shape[1]
    hr = HEADS_PER_GROUP * rows
    row_bits = rows.bit_length() - 1
    head_of_row = lax.shift_right_logical(lax.broadcasted_iota(jnp.int32, (hr, gw), 0), row_bits)
    head_of_lane = lax.broadcasted_iota(jnp.int32, (hr, gw), 1) // ATTN_HEAD_DIM
    blockdiag = head_of_row == head_of_lane
    out_head = lax.broadcasted_iota(jnp.int32, (rows, gw), 1) // ATTN_HEAD_DIM
    t_new = lax.broadcasted_iota(jnp.int32, (hr, rows), 1)
    q_new = lax.bitwise_and(lax.broadcasted_iota(jnp.int32, (hr, rows), 0), rows - 1)
    for b in range(bt):
        o_g, l_g = [], []
        for j, (_, dil) in enumerate(DIL_PATTERNS):
            cache = caches[j]
            n_pos = cache.shape[-1]
            q = qs[j][b]
            qbd = jnp.where(blockdiag, jnp.concatenate([q] * HEADS_PER_GROUP, axis=0), 0.0).astype(BF16)
            kt = cache[b, 0].reshape(gw, n_pos).astype(BF16)
            vt = cache[b, 1].reshape(gw, n_pos).astype(BF16)
            kn = kns[j][b, :, 0:gw].astype(BF16)
            vn = kns[j][b, :, gw:2 * gw].astype(BF16)
            s = _dot(qbd, kt) * scale
            pos = lax.broadcasted_iota(jnp.int32, (hr, n_pos), 1)
            qi = lax.bitwise_and(lax.broadcasted_iota(jnp.int32, (hr, n_pos), 0), rows - 1)
            valid = (pos >= qi) if dil == 1 else (lax.bitwise_and(pos, dil - 1) == qi)
            s = jnp.where(valid, s, NEG_BIG)
            sn = _dot_nt(qbd, kn) * scale
            valid_new = ((t_new <= q_new) if dil == 1 else (t_new == q_new)) & (t_new < seqlen)
            sn = jnp.where(valid_new, sn, NEG_BIG)
            m = jnp.maximum(jnp.max(s, axis=1, keepdims=True), jnp.max(sn, axis=1, keepdims=True))
            p = jnp.exp(s - m)
            pn = jnp.exp(sn - m)
            l = jnp.sum(p, axis=1, keepdims=True) + jnp.sum(pn, axis=1, keepdims=True)
            acc = _dot_nt(p.astype(BF16), vt) + _dot(pn.astype(BF16), vn)
            o_g.append(acc / l)
            l_g.append(m + jnp.log(l))
        mx = jnp.maximum(jnp.maximum(l_g[0], l_g[1]), l_g[2])
        es = [jnp.exp(l - mx) for l in l_g]
        mixed = (es[0] * o_g[0] + es[1] * o_g[1] + es[2] * o_g[2]) / (es[0] + es[1] + es[2])
        out = mixed[(HEADS_PER_GROUP - 1) * rows:, :]
        for h in range(HEADS_PER_GROUP - 1):
            out = jnp.where(out_head == h, mixed[h * rows:(h + 1) * rows, :], out)
        o_ref[b] = out


def _attn_sample(qs, kvs, caches, layer, bsz, seqlen):
    gw = GROUP_WIDTH
    rows = SUBLANES
    bt = ATTN_SAMPLE_BATCH_TILE
    pad = lambda a, w: jnp.pad(a.reshape(bsz, seqlen, w), ((0, 0), (0, rows - seqlen), (0, 0)))
    q8 = [pad(q, gw) for q in qs]
    kv8 = [pad(kv, 2 * gw) for kv in kvs]
    views, cache_specs = [], []
    for cache in caches:
        view = jnp.transpose(cache, (0, 1, 3, 4, 5, 2))
        views.append(view)
        cache_specs.append(pl.BlockSpec((None, bt) + view.shape[2:], lambda i: (layer, i, 0, 0, 0, 0)))
    q_spec = pl.BlockSpec((bt, rows, gw), lambda i: (i, 0, 0))
    kv_spec = pl.BlockSpec((bt, rows, 2 * gw), lambda i: (i, 0, 0))
    o = pl.pallas_call(
        functools.partial(_attn_sample_kernel, seqlen=seqlen),
        grid=(bsz // bt,),
        in_specs=[q_spec] * 3 + [kv_spec] * 3 + cache_specs,
        out_specs=q_spec,
        out_shape=jax.ShapeDtypeStruct((bsz, rows, gw), F32),
        compiler_params=_params(1),
        name="attn_sample",
    )(*q8, *kv8, *views)
    return o[:, :seqlen].reshape(bsz * seqlen, gw)


def _merge_kernel(*refs, alpha, n_o):
    y_ref = refs[0]
    o_refs = refs[1:1 + n_o]
    g_ref, x_ref, wssm_ref, wattn_ref, wo_ref, lg_ref, lb_ref, xo_ref, xob_ref = refs[1 + n_o:]
    d = x_ref.shape[1]
    ys = _dot(y_ref[...].astype(BF16), wssm_ref[...])
    o = jnp.concatenate([r[...] for r in o_refs], axis=1) if n_o > 1 else o_refs[0][...]
    ya = _dot(o.astype(BF16), wattn_ref[...])
    g = g_ref[...].astype(F32)
    merged = g[:, :d] * ys + g[:, d:] * ya
    mix = _dot(merged.astype(BF16), wo_ref[...])
    out = _layer_norm(alpha * x_ref[...] + mix, lg_ref[...], lb_ref[...])
    xo_ref[...] = out
    xob_ref[...] = out.astype(BF16)


def _merge(y_ssm, o_parts, zg, x, w_ssm_br, w_attn_br, w_o, ln_g, ln_b, layer, tm, alpha):
    t, d = x.shape
    d_inner = w_ssm_br.shape[1]
    gw = GROUP_WIDTH
    row = lambda i: (i, 0)
    par = lambda i: (layer, 0, 0)
    return pl.pallas_call(
        functools.partial(_merge_kernel, alpha=alpha, n_o=len(o_parts)),
        grid=(t // tm,),
        in_specs=[pl.BlockSpec((tm, d_inner), row)] +
                 [pl.BlockSpec((tm, o.shape[1]), row) for o in o_parts] +
                 [pl.BlockSpec((tm, 2 * d), lambda i: (i, d_inner // (2 * d))),
                  pl.BlockSpec((tm, d), row),
                  pl.BlockSpec((None, d_inner, d), par),
                  pl.BlockSpec((None, gw, d), par),
                  pl.BlockSpec((None, d, d), par),
                  pl.BlockSpec((None, 1, d), par),
                  pl.BlockSpec((None, 1, d), par)],
        out_specs=[pl.BlockSpec((tm, d), row), pl.BlockSpec((tm, d), row)],
        out_shape=[jax.ShapeDtypeStruct((t, d), F32), jax.ShapeDtypeStruct((t, d), BF16)],
        compiler_params=_params(1),
        name="merge",
    )(y_ssm, *o_parts, zg, x, w_ssm_br, w_attn_br, w_o, ln_g, ln_b)


def _mlp_kernel(xb_ref, x_ref, wup_ref, wdn_ref, lg_ref, lb_ref, xo_ref, xob_ref, acc_ref, *, alpha):
    f = pl.program_id(1)
    nf = pl.num_programs(1)
    hid = _dot(xb_ref[...], wup_ref[...])
    hid = jnp.square(jnp.maximum(hid, 0.0))
    part = _dot(hid.astype(BF16), wdn_ref[...])

    @pl.when(f == 0)
    def _():
        acc_ref[...] = part

    @pl.when(f > 0)
    def _():
        acc_ref[...] += part

    @pl.when(f == nf - 1)
    def _():
        out = _layer_norm(alpha * x_ref[...] + acc_ref[...], lg_ref[...], lb_ref[...])
        xo_ref[...] = out
        xob_ref[...] = out.astype(BF16)


def _mlp(x_bf, x, w_up, w_down, ln_g, ln_b, layer, tm, alpha):
    t, d = x.shape
    d_ff = w_up.shape[2]
    tf = 1024
    return pl.pallas_call(
        functools.partial(_mlp_kernel, alpha=alpha),
        grid=(t // tm, d_ff // tf),
        in_specs=[pl.BlockSpec((tm, d), lambda i, f: (i, 0)),
                  pl.BlockSpec((tm, d), lambda i, f: (i, 0)),
                  pl.BlockSpec((None, d, tf), lambda i, f: (layer, 0, f)),
                  pl.BlockSpec((None, tf, d), lambda i, f: (layer, f, 0)),
                  pl.BlockSpec((None, 1, d), lambda i, f: (layer, 0, 0)),
                  pl.BlockSpec((None, 1, d), lambda i, f: (layer, 0, 0))],
        out_specs=[pl.BlockSpec((tm, d), lambda i, f: (i, 0)), pl.BlockSpec((tm, d), lambda i, f: (i, 0))],
        out_shape=[jax.ShapeDtypeStruct((t, d), F32), jax.ShapeDtypeStruct((t, d), BF16)],
        scratch_shapes=[pltpu.VMEM((tm, d), F32)],
        compiler_params=_params(2),
        name="mlp",
    )(x_bf, x, w_up, w_down, ln_g, ln_b)


def _rope_tables(pos):
    half = ROPE_DIM // 2
    inv_freq = jnp.power(ROPE_THETA, -jnp.arange(half, dtype=F32) / half)
    ang = pos.astype(F32)[:, None] * inv_freq[None, :]
    cos, sin = jnp.cos(ang), jnp.sin(ang)
    n = pos.shape[0]
    pad = jnp.zeros((n, ATTN_HEAD_DIM - ROPE_DIM), F32)
    zero = jnp.zeros((n, half), F32)
    c_head = jnp.concatenate([cos, cos, pad + 1.0], axis=1)
    s1_head = jnp.concatenate([zero, sin, pad], axis=1)
    s2_head = jnp.concatenate([-sin, zero, pad], axis=1)
    rep = LANES // ATTN_HEAD_DIM
    return tuple(jnp.tile(a, (1, rep)) for a in (c_head, s1_head, s2_head))


def kernel(x_prompt, x_sample, cache_kv_w128, cache_kv_w512, cache_kv_w2048, state_conv, state_ssm,
           w_in, conv_w, conv_b, dt_bias, a_log, d_skip, ssm_norm_g, w_ssm_br, w_attn_br, w_o,
           ln1_g, ln1_b, w_up, w_down, ln2_g, ln2_b):
    depth, d_model, _ = w_in.shape
    bsz, seqlen, _ = x_prompt.shape
    dec_b, dec_s, _ = x_sample.shape
    d_inner = w_ssm_br.shape[1]
    conv_dim = conv_w.shape[2]
    n_heads = dt_bias.shape[1]
    attn_w = 3 * GROUP_WIDTH
    gn = SSM_GROUPS * SSM_STATE
    alpha = (2 * depth) ** 0.25

    off_xbc = d_inner
    off_dt = off_xbc + conv_dim
    off_q = off_dt + n_heads
    off_k = off_q + attn_w
    off_v = off_k + attn_w
    off_g = off_v + attn_w

    w_xbc = w_in[:, :, off_xbc:off_dt].astype(BF16)
    w_zg = jnp.concatenate([w_in[:, :, :off_xbc], w_in[:, :, off_g:]], axis=-1).astype(BF16)
    cols = [w_in[:, :, off_q + j * GROUP_WIDTH:off_q + (j + 1) * GROUP_WIDTH] for j in range(3)]
    for j in range(3):
        cols.append(w_in[:, :, off_k + j * GROUP_WIDTH:off_k + (j + 1) * GROUP_WIDTH])
        cols.append(w_in[:, :, off_v + j * GROUP_WIDTH:off_v + (j + 1) * GROUP_WIDTH])
    cols.append(jnp.pad(w_in[:, :, off_dt:off_q], ((0, 0), (0, 0), (0, LANES - n_heads))))
    w_b = jnp.concatenate(cols, axis=-1).astype(BF16)
    w_ssm_bf = w_ssm_br.astype(BF16)
    w_attn_bf = w_attn_br.astype(BF16)
    w_o_bf = w_o.astype(BF16)
    w_up_bf = w_up.astype(BF16)
    w_down_bf = w_down.astype(BF16)

    r3 = lambda a: a.reshape(depth, 1, a.shape[-1])
    dtb = r3(jnp.pad(dt_bias, ((0, 0), (0, LANES - n_heads))))
    alog_pad = r3(jnp.pad(a_log, ((0, 0), (0, LANES - n_heads))))
    dsk_x = r3(jnp.repeat(d_skip, SSM_HEAD_DIM, axis=1))
    conv_b3, ng3 = r3(conv_b), r3(ssm_norm_g)
    ln1_g3, ln1_b3, ln2_g3, ln2_b3 = r3(ln1_g), r3(ln1_b), r3(ln2_g), r3(ln2_b)

    e32 = (jnp.arange(LANES)[:, None] == jnp.arange(d_inner)[None, :] // SSM_HEAD_DIM).astype(BF16)
    eg = (jnp.arange(gn)[:, None] // SSM_STATE ==
          jnp.arange(d_inner)[None, :] // (d_inner // SSM_GROUPS)).astype(BF16)

    rope_p = _rope_tables(jnp.arange(seqlen))
    rope_s = _rope_tables(PAST_LEN + jnp.arange(dec_b * dec_s) % dec_s)

    tp = bsz * seqlen
    ts = dec_b * dec_s
    xp = x_prompt.reshape(tp, d_model)
    xs = x_sample.reshape(ts, d_model)
    xp_bf, xs_bf = xp.astype(BF16), xs.astype(BF16)
    caches = (cache_kv_w128, cache_kv_w512, cache_kv_w2048)

    p_kv = ([], [], [])
    s_kv = ([], [], [])
    p_conv, p_ssm, s_conv, s_ssm = [], [], [], []
    tm_p, tm_s, tn = 1024, ts, 2048
    for l in range(depth):
        xbc_act, tail = _proj_xbc_conv(xp_bf, w_xbc, conv_w, conv_b3, l, tm_p, tn, bsz, seqlen)
        zg = _proj_a(xp_bf, w_zg, l, tm_p, tn, _proj_zg_kernel, BF16, "proj_zg")
        q0, q1, q2, kv0, kv1, kv2, dt = _proj_b(xp_bf, w_b, dtb, rope_p, l, 512)
        y_ssm, h_new = _ssd_prompt(xbc_act, zg, dt, alog_pad, dsk_x, ng3, l, bsz, seqlen)
        o_attn = _attn_prompt((q0, q1, q2), (kv0, kv1, kv2), bsz, seqlen)
        x1, x1_bf = _merge(y_ssm, o_attn, zg, xp, w_ssm_bf, w_attn_bf, w_o_bf, ln1_g3, ln1_b3, l, 512, alpha)
        xp, xp_bf = _mlp(x1_bf, x1, w_up_bf, w_down_bf, ln2_g3, ln2_b3, l, tm_p, alpha)
        for j, ((window, _), kvj) in enumerate(zip(DIL_PATTERNS, (kv0, kv1, kv2))):
            rows = kvj.reshape(bsz, seqlen, 2, HEADS_PER_GROUP, ATTN_HEAD_DIM)
            p_kv[j].append(rows[:, seqlen - min(window, seqlen):])
        p_conv.append(tail[:, SUBLANES - (CONV_WIDTH - 1):, :])
        p_ssm.append(h_new)

        xbc_raw = _proj_a(xs_bf, w_xbc, l, tm_s, tn, _proj_plain_kernel, F32, "proj_xbc")
        zg = _proj_a(xs_bf, w_zg, l, tm_s, tn, _proj_zg_kernel, BF16, "proj_zg")
        q0, q1, q2, kv0, kv1, kv2, dt = _proj_b(xs_bf, w_b, dtb, rope_s, l, tm_s)
        y_ssm, h_new = _ssd_sample(xbc_raw, zg, dt, state_conv, state_ssm, conv_w, conv_b3, alog_pad, dsk_x, ng3,
                                   e32, eg, l, dec_b, dec_s)
        o_attn = _attn_sample((q0, q1, q2), (kv0, kv1, kv2), caches, l, dec_b, dec_s)
        x1, x1_bf = _merge(y_ssm.reshape(ts, d_inner), [o_attn], zg, xs, w_ssm_bf, w_attn_bf, w_o_bf,
                           ln1_g3, ln1_b3, l, tm_s, alpha)
        xs, xs_bf = _mlp(x1_bf, x1, w_up_bf, w_down_bf, ln2_g3, ln2_b3, l, tm_s, alpha)
        for j, kvj in enumerate((kv0, kv1, kv2)):
            s_kv[j].append(kvj.reshape(dec_b, dec_s, 2, HEADS_PER_GROUP, ATTN_HEAD_DIM))
        s_conv.append(xbc_raw.reshape(dec_b, dec_s, conv_dim)[:, dec_s - (CONV_WIDTH - 1):])
        s_ssm.append(h_new)

    return (xp.reshape(bsz, seqlen, d_model), xs.reshape(dec_b, dec_s, d_model),
            jnp.stack(p_kv[0]), jnp.stack(p_kv[1]), jnp.stack(p_kv[2]), jnp.stack(p_conv), jnp.stack(p_ssm),
            jnp.stack(s_kv[0]), jnp.stack(s_kv[1]), jnp.stack(s_kv[2]), jnp.stack(s_conv), jnp.stack(s_ssm))
```

```python
import functools
import math

import jax
import jax.numpy as jnp
from jax import lax
from jax.experimental import pallas as pl
from jax.experimental.pallas import tpu as pltpu

F32 = jnp.float32
BF16 = jnp.bfloat16

SSM_HEAD_DIM = 64
SSM_GROUPS = 8
SSM_STATE = 128
CONV_WIDTH = 4
SSD_CHUNK = 128
CONV_SUB = 256
ATTN_SAMPLE_BATCH_TILE = 4
SSD_SAMPLE_BATCH_TILE = 4
SSD_PROMPT_BATCH_TILE = 1
ATTN_HEAD_DIM = 64
HEADS_PER_GROUP = 4
GROUP_WIDTH = HEADS_PER_GROUP * ATTN_HEAD_DIM
DIL_PATTERNS = ((128, 1), (512, 4), (2048, 16))
ROPE_DIM = 16
ROPE_THETA = 500000.0
LN_EPS = 1e-5
RMS_EPS = 1e-5
PAST_LEN = 2048
NEG_BIG = -1e30

LANES = 128
SUBLANES = 8
V7X_VMEM_LIMIT = 56 * 1024 * 1024


def _params(n_axes):
    return pltpu.CompilerParams(dimension_semantics=("arbitrary",) * n_axes,
                                vmem_limit_bytes=V7X_VMEM_LIMIT)


def _dot(a, b):
    return jnp.dot(a, b, preferred_element_type=F32)


def _dot_nt(a, b):
    return lax.dot_general(a, b, (((1,), (1,)), ((), ())), preferred_element_type=F32)


def _dot_tn(a, b):
    return lax.dot_general(a, b, (((0,), (0,)), ((), ())), preferred_element_type=F32)


def _split3(v):
    hi = v.astype(BF16)
    r1 = v - hi.astype(F32)
    mid = r1.astype(BF16)
    lo = (r1 - mid.astype(F32)).astype(BF16)
    return hi, mid, lo


def _expand_exact(v, e):
    hi, mid, lo = _split3(v)
    return _dot(hi, e) + _dot(mid, e) + _dot(lo, e)


def _stack_rows(rows, n):
    w = rows[0].shape[1]
    ri = lax.broadcasted_iota(jnp.int32, (n, w), 0)
    out = jnp.zeros((n, w), F32)
    for k, row in enumerate(rows):
        out = jnp.where(ri == k, row, out)
    return out


def _silu(v):
    return v * jax.nn.sigmoid(v)


def _layer_norm(r, g, b):
    mu = jnp.mean(r, axis=-1, keepdims=True)
    cen = r - mu
    var = jnp.mean(cen * cen, axis=-1, keepdims=True)
    return cen * lax.rsqrt(var + LN_EPS) * g + b


def _gated_group_norm(y, zs, ng, out_ref, lead):
    hgt = y * zs
    gw = hgt.shape[1] // SSM_GROUPS
    for g in range(SSM_GROUPS):
        seg = hgt[:, g * gw:(g + 1) * gw]
        ms = jnp.mean(seg * seg, axis=-1, keepdims=True)
        val = seg * lax.rsqrt(ms + RMS_EPS) * ng[:, g * gw:(g + 1) * gw]
        out_ref[lead + (slice(None), slice(g * gw, (g + 1) * gw))] = val.astype(out_ref.dtype)


def _proj_plain_kernel(x_ref, w_ref, o_ref):
    o_ref[...] = _dot(x_ref[...], w_ref[...])


def _proj_zg_kernel(x_ref, w_ref, o_ref):
    j = pl.program_id(0)
    w = w_ref[...]
    tm = o_ref.shape[0]
    sub = min(CONV_SUB, tm)
    for s in range(tm // sub):
        acc = _dot(x_ref[s * sub:(s + 1) * sub, :], w)
        sig = jax.nn.sigmoid(acc)
        o_ref[s * sub:(s + 1) * sub, :] = jnp.where(j == 0, acc * sig, sig).astype(o_ref.dtype)


def _proj_a(x_bf, w, layer, tm, tn, body, out_dtype, name):
    t, d = x_bf.shape
    n = w.shape[2]
    return pl.pallas_call(
        body,
        grid=(n // tn, t // tm),
        in_specs=[pl.BlockSpec((tm, d), lambda j, i: (i, 0)),
                  pl.BlockSpec((None, d, tn), lambda j, i: (layer, 0, j))],
        out_specs=pl.BlockSpec((tm, tn), lambda j, i: (i, j)),
        out_shape=jax.ShapeDtypeStruct((t, n), out_dtype),
        compiler_params=_params(2),
        name=name,
    )(x_bf, w)


def _proj_xbc_conv_kernel(x_ref, w_ref, cw_ref, cb_ref, o_ref, tail_ref, carry_ref, *, tiles_per_seq):
    i = pl.program_id(1)
    tm, tn = o_ref.shape
    pos_in_seq = i % tiles_per_seq
    prev = jnp.where(pos_in_seq == 0, 0.0, carry_ref[...])
    cw = cw_ref[...]
    cb = cb_ref[...]
    w = w_ref[...]
    row8 = lax.broadcasted_iota(jnp.int32, (SUBLANES, tn), 0)
    for s in range(tm // CONV_SUB):
        raw = _dot(x_ref[s * CONV_SUB:(s + 1) * CONV_SUB, :], w)
        conv = cb + raw * cw[CONV_WIDTH - 1:CONV_WIDTH, :]
        for k in range(1, CONV_WIDTH):
            sh = pltpu.roll(raw, k, 0)
            top = jnp.where(row8 < k, pltpu.roll(prev, k, 0), sh[0:SUBLANES, :])
            sh = jnp.concatenate([top, sh[SUBLANES:, :]], axis=0)
            conv = conv + sh * cw[CONV_WIDTH - 1 - k:CONV_WIDTH - k, :]
        o_ref[s * CONV_SUB:(s + 1) * CONV_SUB, :] = _silu(conv).astype(o_ref.dtype)
        prev = raw[CONV_SUB - SUBLANES:CONV_SUB, :]
    carry_ref[...] = prev

    @pl.when(pos_in_seq == tiles_per_seq - 1)
    def _():
        tail_ref[0] = prev


def _proj_xbc_conv(x_bf, w, conv_w, conv_b, layer, tm, tn, bsz, seqlen):
    t, d = x_bf.shape
    n = w.shape[2]
    tiles_per_seq = seqlen // tm
    return pl.pallas_call(
        functools.partial(_proj_xbc_conv_kernel, tiles_per_seq=tiles_per_seq),
        grid=(n // tn, t // tm),
        in_specs=[pl.BlockSpec((tm, d), lambda j, i: (i, 0)),
                  pl.BlockSpec((None, d, tn), lambda j, i: (layer, 0, j)),
                  pl.BlockSpec((None, CONV_WIDTH, tn), lambda j, i: (layer, 0, j)),
                  pl.BlockSpec((None, 1, tn), lambda j, i: (layer, 0, j))],
        out_specs=[pl.BlockSpec((tm, tn), lambda j, i: (i, j)),
                   pl.BlockSpec((1, SUBLANES, tn), lambda j, i: (i // tiles_per_seq, 0, j))],
        out_shape=[jax.ShapeDtypeStruct((t, n), BF16),
                   jax.ShapeDtypeStruct((bsz, SUBLANES, n), F32)],
        scratch_shapes=[pltpu.VMEM((SUBLANES, tn), F32)],
        compiler_params=_params(2),
        name="proj_xbc_conv",
    )(x_bf, w, conv_w, conv_b)


def _proj_b_kernel(x_ref, w_ref, c_ref, s1_ref, s2_ref, dtb_ref,
                   q0, q1, q2, kv0, kv1, kv2, dt_ref):
    acc = _dot(x_ref[...], w_ref[...])
    c = c_ref[...]
    s1 = s1_ref[...]
    s2 = s2_ref[...]

    def rot(v):
        return v * c + pltpu.roll(v, 8, 1) * s1 + pltpu.roll(v, LANES - 8, 1) * s2

    qs = (q0, q1, q2)
    kvs = (kv0, kv1, kv2)
    q_total = 3 * GROUP_WIDTH
    for j in range(3):
        for u in range(2):
            lo = j * GROUP_WIDTH + u * LANES
            qs[j][:, u * LANES:(u + 1) * LANES] = rot(acc[:, lo:lo + LANES])
        base = q_total + j * 2 * GROUP_WIDTH
        for u in range(2):
            lo = base + u * LANES
            kvs[j][:, u * LANES:(u + 1) * LANES] = rot(acc[:, lo:lo + LANES])
        kvs[j][:, GROUP_WIDTH:2 * GROUP_WIDTH] = acc[:, base + GROUP_WIDTH:base + 2 * GROUP_WIDTH]
    dt_lo = q_total + 3 * 2 * GROUP_WIDTH
    v = acc[:, dt_lo:dt_lo + LANES] + dtb_ref[...]
    dt_ref[...] = jnp.maximum(v, 0.0) + jnp.log1p(jnp.exp(-jnp.abs(v)))


def _proj_b(x_bf, w_b, dtb, rope, layer, tm):
    t, d = x_bf.shape
    n = w_b.shape[2]
    cos_t, sin1_t, sin2_t = rope
    npos = cos_t.shape[0] // tm
    tab = pl.BlockSpec((tm, LANES), lambda i: (i % npos, 0))
    q_shape = jax.ShapeDtypeStruct((t, GROUP_WIDTH), F32)
    kv_shape = jax.ShapeDtypeStruct((t, 2 * GROUP_WIDTH), F32)
    q_spec = pl.BlockSpec((tm, GROUP_WIDTH), lambda i: (i, 0))
    kv_spec = pl.BlockSpec((tm, 2 * GROUP_WIDTH), lambda i: (i, 0))
    return pl.pallas_call(
        _proj_b_kernel,
        grid=(t // tm,),
        in_specs=[pl.BlockSpec((tm, d), lambda i: (i, 0)),
                  pl.BlockSpec((None, d, n), lambda i: (layer, 0, 0)),
                  tab, tab, tab,
                  pl.BlockSpec((None, 1, LANES), lambda i: (layer, 0, 0))],
        out_specs=[q_spec, q_spec, q_spec, kv_spec, kv_spec, kv_spec,
                   pl.BlockSpec((tm, LANES), lambda i: (i, 0))],
        out_shape=[q_shape, q_shape, q_shape, kv_shape, kv_shape, kv_shape,
                   jax.ShapeDtypeStruct((t, LANES), F32)],
        compiler_params=_params(1),
        name="proj_b",
    )(x_bf, w_b, cos_t, sin1_t, sin2_t, dtb)


def _ssd_prompt_kernel(xbc_ref, zs_ref, dt_ref, alog_ref, dsk_ref, ng_ref,
                       y_ref, h_ref, ht_ref, ybuf_ref, *, n_heads):
    for s in range(xbc_ref.shape[0]):
        _ssd_prompt_stream(xbc_ref.at[s], zs_ref.at[s], dt_ref.at[s], alog_ref, dsk_ref, ng_ref,
                           y_ref.at[s], h_ref.at[s], ht_ref.at[s], ybuf_ref.at[s], n_heads)


def _ssd_prompt_stream(xbc_ref, zs_ref, dt_ref, alog_ref, dsk_ref, ng_ref,
                       y_ref, h_ref, ht_ref, ybuf_ref, n_heads):
    q = SSD_CHUNK
    d_inner = n_heads * SSM_HEAD_DIM
    gn = SSM_GROUPS * SSM_STATE
    gw = d_inner // SSM_GROUPS
    heads_per_group = n_heads // SSM_GROUPS
    c = pl.program_id(1)
    nc = pl.num_programs(1)

    @pl.when(c == 0)
    def _():
        ht_ref[...] = jnp.zeros(ht_ref.shape, F32)

    lane = lax.broadcasted_iota(jnp.int32, (q, LANES), 1)
    dt = jnp.where(lane < n_heads, dt_ref[...], 0.0)
    a = -jnp.exp(alog_ref[...])
    da = dt * a
    ri = lax.broadcasted_iota(jnp.int32, (q, q), 0)
    ci = lax.broadcasted_iota(jnp.int32, (q, q), 1)
    tril = ri >= ci
    acum = lax.dot_general(tril.astype(F32), da, (((1,), (0,)), ((), ())),
                           precision=lax.Precision.HIGHEST, preferred_element_type=F32)
    acum_t = acum.T
    dt_t = dt.T
    coef_t = dt_t * jnp.exp(acum_t[:, q - 1:q] - acum_t)
    cd = jnp.exp(acum[q - 1:q, :])

    glane = lax.broadcasted_iota(jnp.int32, (1, gw), 1)
    hmask = [(glane // SSM_HEAD_DIM) == r for r in range(heads_per_group)]
    zero_bf = jnp.zeros((), BF16)

    for g in range(SSM_GROUPS):
        cg_bf = xbc_ref[:, d_inner + gn + g * SSM_STATE:d_inner + gn + (g + 1) * SSM_STATE]
        bg_bf = xbc_ref[:, d_inner + g * SSM_STATE:d_inner + (g + 1) * SSM_STATE]
        cg = cg_bf.astype(F32)
        bg_t = bg_bf.astype(F32).T
        cb = _dot_nt(cg_bf, bg_bf)
        xg = xbc_ref[:, g * gw:(g + 1) * gw]
        htg = ht_ref[g]
        rhs_full = jnp.concatenate([xg, htg.astype(BF16)], axis=0)
        y_g = jnp.zeros((q, gw), F32)
        st_g = jnp.zeros((SSM_STATE, gw), F32)
        cd_g = jnp.zeros((1, gw), F32)
        for r in range(heads_per_group):
            h = g * heads_per_group + r
            acol = jnp.broadcast_to(acum[:, h:h + 1], (q, q))
            seg = acol - acum_t[h:h + 1, :]
            dec = jnp.exp(jnp.where(tril, seg, -jnp.inf))
            m = cb * dec * dt_t[h:h + 1, :]
            eac = jnp.exp(acol) * cg
            lhs = jnp.concatenate([m, eac], axis=1).astype(BF16)
            rhs = jnp.where(hmask[r], rhs_full, zero_bf)
            y_g = y_g + _dot(lhs, rhs)
            bs = (bg_t * coef_t[h:h + 1, :]).astype(BF16)
            st_g = st_g + _dot(bs, jnp.where(hmask[r], xg, zero_bf))
            cd_g = jnp.where(hmask[r], jnp.broadcast_to(cd[:, h:h + 1], (1, gw)), cd_g)
        ht_ref[g] = htg * cd_g + st_g
        ybuf_ref[:, g * gw:(g + 1) * gw] = y_g

    y = ybuf_ref[...] + dsk_ref[...] * xbc_ref[:, :d_inner].astype(F32)
    _gated_group_norm(y, zs_ref[...].astype(F32), ng_ref[...], y_ref, ())

    @pl.when(c == nc - 1)
    def _():
        pairs_per_group = heads_per_group // 2
        for g in range(SSM_GROUPS):
            for pr in range(pairs_per_group):
                blk = ht_ref[g, :, pr * LANES:(pr + 1) * LANES].T
                h0 = g * heads_per_group + 2 * pr
                h_ref[h0:h0 + 2] = blk.reshape(2, SSM_HEAD_DIM, SSM_STATE)


def _ssd_prompt(xbc_act, zg, dt, alog_pad, dsk_x, ssm_norm_g, layer, bsz, seqlen):
    t, conv_dim = xbc_act.shape
    d_inner = ssm_norm_g.shape[2]
    n_heads = d_inner // SSM_HEAD_DIM
    q = SSD_CHUNK
    nc = seqlen // q
    bt = SSD_PROMPT_BATCH_TILE
    seq3 = lambda a: a.reshape(bsz, seqlen, a.shape[1])
    blk = lambda w: pl.BlockSpec((bt, q, w), lambda b, c: (b, c, 0))
    par = lambda b, c: (layer, 0, 0)
    y, h = pl.pallas_call(
        functools.partial(_ssd_prompt_kernel, n_heads=n_heads),
        grid=(bsz // bt, nc),
        in_specs=[blk(conv_dim),
                  blk(d_inner),
                  blk(LANES),
                  pl.BlockSpec((None, 1, LANES), par),
                  pl.BlockSpec((None, 1, d_inner), par),
                  pl.BlockSpec((None, 1, d_inner), par)],
        out_specs=[blk(d_inner),
                   pl.BlockSpec((bt, n_heads, SSM_HEAD_DIM, SSM_STATE), lambda b, c: (b, 0, 0, 0))],
        out_shape=[jax.ShapeDtypeStruct((bsz, seqlen, d_inner), BF16),
                   jax.ShapeDtypeStruct((bsz, n_heads, SSM_HEAD_DIM, SSM_STATE), F32)],
        scratch_shapes=[pltpu.VMEM((bt, SSM_GROUPS, SSM_STATE, d_inner // SSM_GROUPS), F32),
                        pltpu.VMEM((bt, q, d_inner), F32)],
        compiler_params=_params(2),
        name="ssd_prompt",
    )(seq3(xbc_act), seq3(zg), seq3(dt), alog_pad, dsk_x, ssm_norm_g)
    return y.reshape(t, d_inner), h


def _ssd_sample_kernel(xbc_ref, zs_ref, dt_ref, sc_ref, h0_ref, cw_ref, cb_ref, alog_ref, dsk_ref, ng_ref,
                       e32_ref, eg_ref, y_ref, h_ref, *, n_heads, seqlen):
    for s in range(xbc_ref.shape[0]):
        _ssd_sample_one(s, xbc_ref, zs_ref, dt_ref, sc_ref, h0_ref, cw_ref, cb_ref, alog_ref, dsk_ref, ng_ref,
                        e32_ref, eg_ref, y_ref, h_ref, n_heads, seqlen)


def _ssd_sample_one(s, xbc_ref, zs_ref, dt_ref, sc_ref, h0_ref, cw_ref, cb_ref, alog_ref, dsk_ref, ng_ref,
                    e32_ref, eg_ref, y_ref, h_ref, n_heads, seqlen):
    d_inner = n_heads * SSM_HEAD_DIM
    gn = SSM_GROUPS * SSM_STATE
    sc = sc_ref[s]
    x = xbc_ref[s]
    xp = [sc[s:s + 1, :] for s in range(CONV_WIDTH - 1)] + [x[s:s + 1, :] for s in range(seqlen)]
    cw = cw_ref[...]
    cb = cb_ref[...]
    xs, bm, cm = [], [], []
    for t in range(seqlen):
        acc = cb
        for tap in range(CONV_WIDTH):
            acc = acc + xp[t + tap] * cw[tap:tap + 1, :]
        act = _silu(acc)
        xs.append(act[:, :d_inner])
        bm.append(act[:, d_inner:d_inner + gn])
        cm.append(act[:, d_inner + gn:])

    lane = lax.broadcasted_iota(jnp.int32, (1, LANES), 1)
    dtv = dt_ref[s]
    dt = [jnp.where(lane < n_heads, dtv[t:t + 1, :], 0.0) for t in range(seqlen)]
    a = -jnp.exp(alog_ref[...])
    acum = []
    run = jnp.zeros((1, LANES), F32)
    for t in range(seqlen):
        run = run + dt[t] * a
        acum.append(run)
    last = seqlen - 1

    pairs = [(t, j) for t in range(seqlen) for j in range(t + 1)]
    rows = [jnp.exp(acum[t] - acum[j]) * dt[j] for (t, j) in pairs]
    ea_row0 = len(rows)
    rows += [jnp.exp(acum[t]) for t in range(seqlen)]
    ce_row0 = len(rows)
    rows += [jnp.exp(acum[last] - acum[j]) * dt[j] for j in range(seqlen)]
    n_rows = -(-len(rows) // 8) * 8
    rx = _expand_exact(_stack_rows(rows, n_rows), e32_ref[...])

    n_prod = -(-len(pairs) // 8) * 8
    prods = _stack_rows([cm[t] * bm[j] for (t, j) in pairs], n_prod)
    p_hi = prods.astype(BF16)
    p_lo = (prods - p_hi.astype(F32)).astype(BF16)
    eg = eg_ref[...]
    cbx = _dot(p_hi, eg) + _dot(p_lo, eg)

    crow = _stack_rows(cm, 8).astype(BF16)
    n_pairs = n_heads // 2
    heads_per_group = n_heads // SSM_GROUPS
    yoff_parts = []
    for k in range(n_pairs):
        g = (2 * k) // heads_per_group
        hp = h0_ref[s, 2 * k:2 * k + 2].reshape(2 * SSM_HEAD_DIM, SSM_STATE)
        yoff_parts.append(_dot_nt(crow[:, g * SSM_STATE:(g + 1) * SSM_STATE], hp.astype(BF16)))
    yoff = jnp.concatenate(yoff_parts, axis=1)

    dsk = dsk_ref[...]
    ys = []
    for t in range(seqlen):
        yt = rx[ea_row0 + t:ea_row0 + t + 1, :] * yoff[t:t + 1, :] + dsk * xs[t]
        for idx, (tt, j) in enumerate(pairs):
            if tt == t:
                yt = yt + cbx[idx:idx + 1, :] * rx[idx:idx + 1, :] * xs[j]
        ys.append(yt)
    y = _stack_rows(ys, seqlen)
    _gated_group_norm(y, zs_ref[s].astype(F32), ng_ref[...], y_ref, (s,))

    cdx = rx[ea_row0 + last:ea_row0 + last + 1, :]
    cd_hi = cdx.astype(BF16).astype(F32)
    cd_r = cdx - cd_hi
    cd_mid = cd_r.astype(BF16).astype(F32)
    cd_lo = cd_r - cd_mid
    lhs_rows = [rx[ce_row0 + j:ce_row0 + j + 1, :] * xs[j] for j in range(seqlen)] + [cd_hi, cd_mid, cd_lo]
    lhs = _stack_rows(lhs_rows, 8).astype(BF16)
    zero_n = jnp.zeros((1, SSM_STATE), F32)
    one_n = jnp.ones((1, SSM_STATE), F32)
    rhs_by_group = []
    for g in range(SSM_GROUPS):
        rws = [jnp.concatenate([bm[j][:, g * SSM_STATE:(g + 1) * SSM_STATE], zero_n], axis=1)
               for j in range(seqlen)]
        rws += [jnp.concatenate([zero_n, one_n], axis=1)] * 3
        rhs_by_group.append(_stack_rows(rws, 8).astype(BF16))
    for k in range(n_pairs):
        g = (2 * k) // heads_per_group
        out = _dot_tn(lhs[:, k * LANES:(k + 1) * LANES], rhs_by_group[g])
        hp = h0_ref[s, 2 * k:2 * k + 2].reshape(2 * SSM_HEAD_DIM, SSM_STATE)
        hn = out[:, SSM_STATE:] * hp + out[:, :SSM_STATE]
        h_ref[s, 2 * k:2 * k + 2] = hn.reshape(2, SSM_HEAD_DIM, SSM_STATE)


def _ssd_sample(xbc_raw, zg, dt, state_conv, state_ssm, conv_w, conv_b, alog_pad, dsk_x, ssm_norm_g,
                e32, eg, layer, bsz, seqlen):
    conv_dim = conv_w.shape[2]
    d_inner = ssm_norm_g.shape[2]
    n_heads = d_inner // SSM_HEAD_DIM
    x3 = xbc_raw.reshape(bsz, seqlen, conv_dim)
    zg3 = zg.reshape(bsz, seqlen, zg.shape[1])
    dt3 = dt.reshape(bsz, seqlen, LANES)
    par = lambda b: (layer, 0, 0)
    bt = SSD_SAMPLE_BATCH_TILE
    return pl.pallas_call(
        functools.partial(_ssd_sample_kernel, n_heads=n_heads, seqlen=seqlen),
        grid=(bsz // bt,),
        in_specs=[pl.BlockSpec((bt, seqlen, conv_dim), lambda b: (b, 0, 0)),
                  pl.BlockSpec((bt, seqlen, d_inner), lambda b: (b, 0, 0)),
                  pl.BlockSpec((bt, seqlen, LANES), lambda b: (b, 0, 0)),
                  pl.BlockSpec((None, bt, CONV_WIDTH - 1, conv_dim), lambda b: (layer, b, 0, 0)),
                  pl.BlockSpec((None, bt, n_heads, SSM_HEAD_DIM, SSM_STATE), lambda b: (layer, b, 0, 0, 0)),
                  pl.BlockSpec((None, CONV_WIDTH, conv_dim), par),
                  pl.BlockSpec((None, 1, conv_dim), par),
                  pl.BlockSpec((None, 1, LANES), par),
                  pl.BlockSpec((None, 1, d_inner), par),
                  pl.BlockSpec((None, 1, d_inner), par),
                  pl.BlockSpec(e32.shape, lambda b: (0, 0)),
                  pl.BlockSpec(eg.shape, lambda b: (0, 0))],
        out_specs=[pl.BlockSpec((bt, seqlen, d_inner), lambda b: (b, 0, 0)),
                   pl.BlockSpec((bt, n_heads, SSM_HEAD_DIM, SSM_STATE), lambda b: (b, 0, 0, 0))],
        out_shape=[jax.ShapeDtypeStruct((bsz, seqlen, d_inner), F32),
                   jax.ShapeDtypeStruct((bsz, n_heads, SSM_HEAD_DIM, SSM_STATE), F32)],
        compiler_params=_params(1),
        name="ssd_sample",
    )(x3, zg3, dt3, state_conv, state_ssm, conv_w, conv_b, alog_pad, dsk_x, ssm_norm_g, e32, eg)


def _mix_into(o_ref, lse_ref, idx, o_new, lse_new, first):
    if first:
        o_ref[idx] = o_new
        lse_ref[idx[1:]] = lse_new
        return
    o_old = o_ref[idx]
    l_old = lse_ref[idx[1:]]
    mx = jnp.maximum(l_old, lse_new)
    e0 = jnp.exp(l_old - mx)
    e1 = jnp.exp(lse_new - mx)
    s = e0 + e1
    o_ref[idx] = (e0 * o_old + e1 * o_new) / s
    lse_ref[idx[1:]] = mx + jnp.log(s)


def _attn_prompt_kernel(*refs, seqlen):
    tq = 128
    n_chunks = GROUP_WIDTH // LANES
    heads_per_chunk = LANES // ATTN_HEAD_DIM
    per_group = 3 * n_chunks
    n_in = per_group * len(DIL_PATTERNS)
    ins, o_refs, lse_refs = refs[:n_in], refs[n_in:n_in + n_chunks], refs[n_in + n_chunks:]
    scale = 1.0 / math.sqrt(ATTN_HEAD_DIM)
    lane = lax.broadcasted_iota(jnp.int32, (1, LANES), 1)
    hmask = [(lane // ATTN_HEAD_DIM) == e for e in range(heads_per_chunk)]

    for j, (_, dil) in enumerate(DIL_PATTERNS):
        grp = ins[per_group * j:per_group * (j + 1)]
        sub_len = seqlen // dil
        n_tiles = sub_len // tq
        win = min(2 * tq, sub_len)
        tile_bits = n_tiles.bit_length() - 1

        def unit(u, carry, j=j, dil=dil, grp=grp, n_tiles=n_tiles, win=win, tile_bits=tile_bits):
            r = lax.shift_right_logical(u, tile_bits)
            ti = lax.bitwise_and(u, n_tiles - 1)
            m0 = ti * tq
            ks = jnp.maximum(m0 - (win - tq), 0)
            if dil == 1:
                rows_q = pl.ds(pl.multiple_of(m0, tq), tq)
                rows_k = pl.ds(pl.multiple_of(ks, tq), win)
            else:
                rows_q = pl.ds(r + dil * m0, tq, stride=dil)
                rows_k = pl.ds(r + dil * ks, win, stride=dil)
            qpos = m0 + lax.bitwise_and(lax.broadcasted_iota(jnp.int32, (heads_per_chunk * tq, win), 0), tq - 1)
            kpos = ks + lax.broadcasted_iota(jnp.int32, (heads_per_chunk * tq, win), 1)
            valid = (kpos <= qpos) & (kpos >= qpos - 128)
            for c in range(n_chunks):
                q = grp[c][0, rows_q, :]
                k = grp[n_chunks + c][0, rows_k, :].astype(BF16)
                v = grp[2 * n_chunks + c][0, rows_k, :].astype(BF16)
                qbd = jnp.concatenate([jnp.where(hmask[e], q, 0.0) for e in range(heads_per_chunk)],
                                      axis=0).astype(BF16)
                s = _dot_nt(qbd, k) * scale
                s = jnp.where(valid, s, NEG_BIG)
                m = jnp.max(s, axis=1, keepdims=True)
                p = jnp.exp(s - m)
                l = jnp.sum(p, axis=1, keepdims=True)
                acc = _dot((p / l).astype(BF16), v)
                lse = jnp.broadcast_to(m + jnp.log(l), acc.shape)
                o_acc = acc[(heads_per_chunk - 1) * tq:, :]
                lse_acc = lse[(heads_per_chunk - 1) * tq:, :]
                for e in range(heads_per_chunk - 1):
                    o_acc = jnp.where(hmask[e], acc[e * tq:(e + 1) * tq, :], o_acc)
                    lse_acc = jnp.where(hmask[e], lse[e * tq:(e + 1) * tq, :], lse_acc)
                _mix_into(o_refs[c], lse_refs[c], (0, rows_q, slice(None)), o_acc, lse_acc, j == 0)
            return carry

        lax.fori_loop(0, dil * n_tiles, unit, 0, unroll=4)


def _attn_prompt(qs, kvs, bsz, seqlen):
    gw = GROUP_WIDTH
    n_chunks = gw // LANES
    chunk = lambda c: pl.BlockSpec((1, seqlen, LANES), lambda b: (b, 0, c))
    args, specs = [], []
    for q, kv in zip(qs, kvs):
        q3 = q.reshape(bsz, seqlen, gw)
        kv3 = kv.reshape(bsz, seqlen, 2 * gw)
        args += [q3] * n_chunks + [kv3] * (2 * n_chunks)
        specs += [chunk(c) for c in range(n_chunks)] + [chunk(c) for c in range(2 * n_chunks)]
    out_spec = pl.BlockSpec((1, seqlen, LANES), lambda b: (b, 0, 0))
    shape = jax.ShapeDtypeStruct((bsz, seqlen, LANES), F32)
    outs = pl.pallas_call(
        functools.partial(_attn_prompt_kernel, seqlen=seqlen),
        grid=(bsz,),
        in_specs=specs,
        out_specs=[out_spec] * n_chunks,
        out_shape=[shape] * n_chunks,
        scratch_shapes=[pltpu.VMEM((seqlen, LANES), F32)] * n_chunks,
        compiler_params=_params(1),
        name="attn_prompt",
    )(*args)
    return [o.reshape(bsz * seqlen, LANES) for o in outs]


def _attn_sample_kernel(q0, q1, q2, kn0, kn1, kn2, c0, c1, c2, o_ref, *, seqlen):
    scale = 1.0 / math.sqrt(ATTN_HEAD_DIM)
    gw = GROUP_WIDTH
    qs, kns, caches = (q0, q1, q2), (kn0, kn1, kn2), (c0, c1, c2)
    bt, rows = q0.shape[0], q0.shape[1]
    hr = HEADS_PER_GROUP * rows
    row_bits = rows.bit_length() - 1
    head_of_row = lax.shift_right_logical(lax.broadcasted_iota(jnp.int32, (hr, gw), 0), row_bits)
    head_of_lane = lax.broadcasted_iota(jnp.int32, (hr, gw), 1) // ATTN_HEAD_DIM
    blockdiag = head_of_row == head_of_lane
    out_head = lax.broadcasted_iota(jnp.int32, (rows, gw), 1) // ATTN_HEAD_DIM
    t_new = lax.broadcasted_iota(jnp.int32, (hr, rows), 1)
    q_new = lax.bitwise_and(lax.broadcasted_iota(jnp.int32, (hr, rows), 0), rows - 1)
    for b in range(bt):
        o_g, l_g = [], []
        for j, (_, dil) in enumerate(DIL_PATTERNS):
            cache = caches[j]
            n_pos = cache.shape[-1]
            q = qs[j][b]
            qbd = jnp.where(blockdiag, jnp.concatenate([q] * HEADS_PER_GROUP, axis=0), 0.0).astype(BF16)
            kt = cache[b, 0].reshape(gw, n_pos).astype(BF16)
            vt = cache[b, 1].reshape(gw, n_pos).astype(BF16)
            kn = kns[j][b, :, 0:gw].astype(BF16)
            vn = kns[j][b, :, gw:2 * gw].astype(BF16)
            s = _dot(qbd, kt) * scale
            pos = lax.broadcasted_iota(jnp.int32, (hr, n_pos), 1)
            qi = lax.bitwise_and(lax.broadcasted_iota(jnp.int32, (hr, n_pos), 0), rows - 1)
            valid = (pos >= qi) if dil == 1 else (lax.bitwise_and(pos, dil - 1) == qi)
            s = jnp.where(valid, s, NEG_BIG)
            sn = _dot_nt(qbd, kn) * scale
            valid_new = ((t_new <= q_new) if dil == 1 else (t_new == q_new)) & (t_new < seqlen)
            sn = jnp.where(valid_new, sn, NEG_BIG)
            m = jnp.maximum(jnp.max(s, axis=1, keepdims=True), jnp.max(sn, axis=1, keepdims=True))
            p = jnp.exp(s - m)
            pn = jnp.exp(sn - m)
            l = jnp.sum(p, axis=1, keepdims=True) + jnp.sum(pn, axis=1, keepdims=True)
            acc = _dot_nt(p.astype(BF16), vt) + _dot(pn.astype(BF16), vn)
            o_g.append(acc / l)
            l_g.append(m + jnp.log(l))
        mx = jnp.maximum(jnp.maximum(l_g[0], l_g[1]), l_g[2])
        es = [jnp.exp(l - mx) for l in l_g]
        mixed = (es[0] * o_g[0] + es[1] * o_g[1] + es[2] * o_g[2]) / (es[0] + es[1] + es[2])
        out = mixed[(HEADS_PER_GROUP - 1) * rows:, :]
        for h in range(HEADS_PER_GROUP - 1):
            out = jnp.where(out_head == h, mixed[h * rows:(h + 1) * rows, :], out)
        o_ref[b] = out


def _attn_sample(qs, kvs, caches, layer, bsz, seqlen):
    gw = GROUP_WIDTH
    rows = SUBLANES
    bt = ATTN_SAMPLE_BATCH_TILE
    pad = lambda a, w: jnp.pad(a.reshape(bsz, seqlen, w), ((0, 0), (0, rows - seqlen), (0, 0)))
    q8 = [pad(q, gw) for q in qs]
    kv8 = [pad(kv, 2 * gw) for kv in kvs]
    views, cache_specs = [], []
    for cache in caches:
        view = jnp.transpose(cache, (0, 1, 3, 4, 5, 2))
        views.append(view)
        cache_specs.append(pl.BlockSpec((None, bt) + view.shape[2:], lambda i: (layer, i, 0, 0, 0, 0)))
    q_spec = pl.BlockSpec((bt, rows, gw), lambda i: (i, 0, 0))
    kv_spec = pl.BlockSpec((bt, rows, 2 * gw), lambda i: (i, 0, 0))
    o = pl.pallas_call(
        functools.partial(_attn_sample_kernel, seqlen=seqlen),
        grid=(bsz // bt,),
        in_specs=[q_spec] * 3 + [kv_spec] * 3 + cache_specs,
        out_specs=q_spec,
        out_shape=jax.ShapeDtypeStruct((bsz, rows, gw), F32),
        compiler_params=_params(1),
        name="attn_sample",
    )(*q8, *kv8, *views)
    return o[:, :seqlen].reshape(bsz * seqlen, gw)


def _merge_kernel(*refs, alpha, n_o):
    y_ref = refs[0]
    o_refs = refs[1:1 + n_o]
    g_ref, x_ref, wssm_ref, wattn_ref, wo_ref, lg_ref, lb_ref, xo_ref, xob_ref = refs[1 + n_o:]
    d = x_ref.shape[1]
    ys = _dot(y_ref[...].astype(BF16), wssm_ref[...])
    o = jnp.concatenate([r[...] for r in o_refs], axis=1) if n_o > 1 else o_refs[0][...]
    ya = _dot(o.astype(BF16), wattn_ref[...])
    g = g_ref[...].astype(F32)
    merged = g[:, :d] * ys + g[:, d:] * ya
    mix = _dot(merged.astype(BF16), wo_ref[...])
    out = _layer_norm(alpha * x_ref[...] + mix, lg_ref[...], lb_ref[...])
    xo_ref[...] = out
    xob_ref[...] = out.astype(BF16)


def _merge(y_ssm, o_parts, zg, x, w_ssm_br, w_attn_br, w_o, ln_g, ln_b, layer, tm, alpha):
    t, d = x.shape
    d_inner = w_ssm_br.shape[1]
    gw = GROUP_WIDTH
    row = lambda i: (i, 0)
    par = lambda i: (layer, 0, 0)
    return pl.pallas_call(
        functools.partial(_merge_kernel, alpha=alpha, n_o=len(o_parts)),
        grid=(t // tm,),
        in_specs=[pl.BlockSpec((tm, d_inner), row)] +
                 [pl.BlockSpec((tm, o.shape[1]), row) for o in o_parts] +
                 [pl.BlockSpec((tm, 2 * d), lambda i: (i, d_inner // (2 * d))),
                  pl.BlockSpec((tm, d), row),
                  pl.BlockSpec((None, d_inner, d), par),
                  pl.BlockSpec((None, gw, d), par),
                  pl.BlockSpec((None, d, d), par),
                  pl.BlockSpec((None, 1, d), par),
                  pl.BlockSpec((None, 1, d), par)],
        out_specs=[pl.BlockSpec((tm, d), row), pl.BlockSpec((tm, d), row)],
        out_shape=[jax.ShapeDtypeStruct((t, d), F32), jax.ShapeDtypeStruct((t, d), BF16)],
        compiler_params=_params(1),
        name="merge",
    )(y_ssm, *o_parts, zg, x, w_ssm_br, w_attn_br, w_o, ln_g, ln_b)


def _mlp_kernel(xb_ref, x_ref, wup_ref, wdn_ref, lg_ref, lb_ref, xo_ref, xob_ref, acc_ref, *, alpha):
    f = pl.program_id(1)
    nf = pl.num_programs(1)
    hid = _dot(xb_ref[...], wup_ref[...])
    hid = jnp.square(jnp.maximum(hid, 0.0))
    part = _dot(hid.astype(BF16), wdn_ref[...])

    @pl.when(f == 0)
    def _():
        acc_ref[...] = part

    @pl.when(f > 0)
    def _():
        acc_ref[...] += part

    @pl.when(f == nf - 1)
    def _():
        out = _layer_norm(alpha * x_ref[...] + acc_ref[...], lg_ref[...], lb_ref[...])
        xo_ref[...] = out
        xob_ref[...] = out.astype(BF16)


def _mlp(x_bf, x, w_up, w_down, ln_g, ln_b, layer, tm, alpha):
    t, d = x.shape
    d_ff = w_up.shape[2]
    tf = 1024
    return pl.pallas_call(
        functools.partial(_mlp_kernel, alpha=alpha),
        grid=(t // tm, d_ff // tf),
        in_specs=[pl.BlockSpec((tm, d), lambda i, f: (i, 0)),
                  pl.BlockSpec((tm, d), lambda i, f: (i, 0)),
                  pl.BlockSpec((None, d, tf), lambda i, f: (layer, 0, f)),
                  pl.BlockSpec((None, tf, d), lambda i, f: (layer, f, 0)),
                  pl.BlockSpec((None, 1, d), lambda i, f: (layer, 0, 0)),
                  pl.BlockSpec((None, 1, d), lambda i, f: (layer, 0, 0))],
        out_specs=[pl.BlockSpec((tm, d), lambda i, f: (i, 0)), pl.BlockSpec((tm, d), lambda i, f: (i, 0))],
        out_shape=[jax.ShapeDtypeStruct((t, d), F32), jax.ShapeDtypeStruct((t, d), BF16)],
        scratch_shapes=[pltpu.VMEM((tm, d), F32)],
        compiler_params=_params(2),
        name="mlp",
    )(x_bf, x, w_up, w_down, ln_g, ln_b)


def _rope_tables(pos):
    half = ROPE_DIM // 2
    inv_freq = jnp.power(ROPE_THETA, -jnp.arange(half, dtype=F32) / half)
    ang = pos.astype(F32)[:, None] * inv_freq[None, :]
    cos, sin = jnp.cos(ang), jnp.sin(ang)
    n = pos.shape[0]
    pad = jnp.zeros((n, ATTN_HEAD_DIM - ROPE_DIM), F32)
    zero = jnp.zeros((n, half), F32)
    c_head = jnp.concatenate([cos, cos, pad + 1.0], axis=1)
    s1_head = jnp.concatenate([zero, sin, pad], axis=1)
    s2_head = jnp.concatenate([-sin, zero, pad], axis=1)
    rep = LANES // ATTN_HEAD_DIM
    return tuple(jnp.tile(a, (1, rep)) for a in (c_head, s1_head, s2_head))


def kernel(x_prompt, x_sample, cache_kv_w128, cache_kv_w512, cache_kv_w2048, state_conv, state_ssm,
           w_in, conv_w, conv_b, dt_bias, a_log, d_skip, ssm_norm_g, w_ssm_br, w_attn_br, w_o,
           ln1_g, ln1_b, w_up, w_down, ln2_g, ln2_b):
    depth, d_model, _ = w_in.shape
    bsz, seqlen, _ = x_prompt.shape
    dec_b, dec_s, _ = x_sample.shape
    d_inner = w_ssm_br.shape[1]
    conv_dim = conv_w.shape[2]
    n_heads = dt_bias.shape[1]
    attn_w = 3 * GROUP_WIDTH
    gn = SSM_GROUPS * SSM_STATE
    alpha = (2 * depth) ** 0.25

    off_xbc = d_inner
    off_dt = off_xbc + conv_dim
    off_q = off_dt + n_heads
    off_k = off_q + attn_w
    off_v = off_k + attn_w
    off_g = off_v + attn_w

    w_xbc = w_in[:, :, off_xbc:off_dt].astype(BF16)
    w_zg = jnp.concatenate([w_in[:, :, :off_xbc], w_in[:, :, off_g:]], axis=-1).astype(BF16)
    cols = [w_in[:, :, off_q + j * GROUP_WIDTH:off_q + (j + 1) * GROUP_WIDTH] for j in range(3)]
    for j in range(3):
        cols.append(w_in[:, :, off_k + j * GROUP_WIDTH:off_k + (j + 1) * GROUP_WIDTH])
        cols.append(w_in[:, :, off_v + j * GROUP_WIDTH:off_v + (j + 1) * GROUP_WIDTH])
    cols.append(jnp.pad(w_in[:, :, off_dt:off_q], ((0, 0), (0, 0), (0, LANES - n_heads))))
    w_b = jnp.concatenate(cols, axis=-1).astype(BF16)
    w_ssm_bf = w_ssm_br.astype(BF16)
    w_attn_bf = w_attn_br.astype(BF16)
    w_o_bf = w_o.astype(BF16)
    w_up_bf = w_up.astype(BF16)
    w_down_bf = w_down.astype(BF16)

    r3 = lambda a: a.reshape(depth, 1, a.shape[-1])
    dtb = r3(jnp.pad(dt_bias, ((0, 0), (0, LANES - n_heads))))
    alog_pad = r3(jnp.pad(a_log, ((0, 0), (0, LANES - n_heads))))
    dsk_x = r3(jnp.repeat(d_skip, SSM_HEAD_DIM, axis=1))
    conv_b3, ng3 = r3(conv_b), r3(ssm_norm_g)
    ln1_g3, ln1_b3, ln2_g3, ln2_b3 = r3(ln1_g), r3(ln1_b), r3(ln2_g), r3(ln2_b)

    e32 = (jnp.arange(LANES)[:, None] == jnp.arange(d_inner)[None, :] // SSM_HEAD_DIM).astype(BF16)
    eg = (jnp.arange(gn)[:, None] // SSM_STATE ==
          jnp.arange(d_inner)[None, :] // (d_inner // SSM_GROUPS)).astype(BF16)

    rope_p = _rope_tables(jnp.arange(seqlen))
    rope_s = _rope_tables(PAST_LEN + jnp.arange(dec_b * dec_s) % dec_s)

    tp = bsz * seqlen
    ts = dec_b * dec_s
    xp = x_prompt.reshape(tp, d_model)
    xs = x_sample.reshape(ts, d_model)
    xp_bf, xs_bf = xp.astype(BF16), xs.astype(BF16)
    caches = (cache_kv_w128, cache_kv_w512, cache_kv_w2048)

    p_kv = ([], [], [])
    s_kv = ([], [], [])
    p_conv, p_ssm, s_conv, s_ssm = [], [], [], []
    tm_p, tm_s, tn = 1024, ts, 2048
    for l in range(depth):
        xbc_act, tail = _proj_xbc_conv(xp_bf, w_xbc, conv_w, conv_b3, l, tm_p, tn, bsz, seqlen)
        zg = _proj_a(xp_bf, w_zg, l, tm_p, tn, _proj_zg_kernel, BF16, "proj_zg")
        q0, q1, q2, kv0, kv1, kv2, dt = _proj_b(xp_bf, w_b, dtb, rope_p, l, 512)
        y_ssm, h_new = _ssd_prompt(xbc_act, zg, dt, alog_pad, dsk_x, ng3, l, bsz, seqlen)
        o_attn = _attn_prompt((q0, q1, q2), (kv0, kv1, kv2), bsz, seqlen)
        x1, x1_bf = _merge(y_ssm, o_attn, zg, xp, w_ssm_bf, w_attn_bf, w_o_bf, ln1_g3, ln1_b3, l, 512, alpha)
        xp, xp_bf = _mlp(x1_bf, x1, w_up_bf, w_down_bf, ln2_g3, ln2_b3, l, tm_p, alpha)
        for j, ((window, _), kvj) in enumerate(zip(DIL_PATTERNS, (kv0, kv1, kv2))):
            rows = kvj.reshape(bsz, seqlen, 2, HEADS_PER_GROUP, ATTN_HEAD_DIM)
            p_kv[j].append(rows[:, seqlen - min(window, seqlen):])
        p_conv.append(tail[:, SUBLANES - (CONV_WIDTH - 1):, :])
        p_ssm.append(h_new)

        xbc_raw = _proj_a(xs_bf, w_xbc, l, tm_s, tn, _proj_plain_kernel, F32, "proj_xbc")
        zg = _proj_a(xs_bf, w_zg, l, tm_s, tn, _proj_zg_kernel, BF16, "proj_zg")
        q0, q1, q2, kv0, kv1, kv2, dt = _proj_b(xs_bf, w_b, dtb, rope_s, l, tm_s)
        y_ssm, h_new = _ssd_sample(xbc_raw, zg, dt, state_conv, state_ssm, conv_w, conv_b3, alog_pad, dsk_x, ng3,
                                   e32, eg, l, dec_b, dec_s)
        o_attn = _attn_sample((q0, q1, q2), (kv0, kv1, kv2), caches, l, dec_b, dec_s)
        x1, x1_bf = _merge(y_ssm.reshape(ts, d_inner), [o_attn], zg, xs, w_ssm_bf, w_attn_bf, w_o_bf,
                           ln1_g3, ln1_b3, l, tm_s, alpha)
        xs, xs_bf = _mlp(x1_bf, x1, w_up_bf, w_down_bf, ln2_g3, ln2_b3, l, tm_s, alpha)
        for j, kvj in enumerate((kv0, kv1, kv2)):
            s_kv[j].append(kvj.reshape(dec_b, dec_s, 2, HEADS_PER_GROUP, ATTN_HEAD_DIM))
        s_conv.append(xbc_raw.reshape(dec_b, dec_s, conv_dim)[:, dec_s - (CONV_WIDTH - 1):])
        s_ssm.append(h_new)

    return (xp.reshape(bsz, seqlen, d_model), xs.reshape(dec_b, dec_s, d_model),
            jnp.stack(p_kv[0]), jnp.stack(p_kv[1]), jnp.stack(p_kv[2]), jnp.stack(p_conv), jnp.stack(p_ssm),
            jnp.stack(s_kv[0]), jnp.stack(s_kv[1]), jnp.stack(s_kv[2]), jnp.stack(s_conv), jnp.stack(s_ssm))
```
